```python
import math
import jax, jax.numpy as jnp
from jax import lax
import numpy as np

D_MODEL = 1024
BATCH = 4
SEQ = 4096
DEPTH = 1

A_HEADS = 8
A_HEAD_DIM = 64
A_V_DIM = 2 * A_HEAD_DIM
A_QK_W = A_HEADS * 2 * A_HEAD_DIM
A_V_W = A_HEADS * A_V_DIM
LAMBDA_INIT_STD = 0.1
Q_BLOCK = 128
B_HEADS = 8
B_HEAD_DIM = 128
B_W = B_HEADS * B_HEAD_DIM
CONV_WIDTH = 4
CHUNK = 64
D_FF = 2816
FFN_CONV_WIDTH = 3
ROPE_THETA = 10000.0
NORM_EPS = 1e-6
IN_SIZES = (A_QK_W, A_QK_W, A_V_W, B_W, B_W, B_W, B_W, B_HEADS, B_HEADS, D_MODEL, D_MODEL)
D_IN = A_QK_W * 2 + A_V_W + B_W * 4 + B_HEADS * 2 + D_MODEL * 2

kernel_name = "hybrid_diffattn_gdn_convffn_block"


def rms_norm(x, w):
    xf = x.astype(jnp.float32)
    y = xf * lax.rsqrt(jnp.mean(xf * xf, axis=-1, keepdims=True) + NORM_EPS)
    return (y * w.astype(jnp.float32)).astype(x.dtype)


def l2_normalize(x):
    xf = x.astype(jnp.float32)
    return xf * lax.rsqrt(jnp.sum(xf * xf, axis=-1, keepdims=True) + NORM_EPS)


def causal_dwconv(x, w):
    k_w, c = w.shape
    return lax.conv_general_dilated(
        x, w[:, None, :].astype(x.dtype), window_strides=(1,), padding=[(k_w - 1, 0)],
        dimension_numbers=("NWC", "WIO", "NWC"), feature_group_count=c)


def rotary(x, positions):
    d = x.shape[-1]
    inv_freq = ROPE_THETA ** (-jnp.arange(0, d, 2, dtype=jnp.float32) / d)
    ang = positions.astype(jnp.float32)[..., None] * inv_freq
    cos = jnp.cos(ang)[:, :, None, :]
    sin = jnp.sin(ang)[:, :, None, :]
    xf = x.astype(jnp.float32)
    x1, x2 = xf[..., : d // 2], xf[..., d // 2:]
    return jnp.concatenate([x1 * cos - x2 * sin, x2 * cos + x1 * sin], axis=-1).astype(x.dtype)


def diff_attention(q, k, v, lam):
    b, s, h, _, d = q.shape
    nb = s // Q_BLOCK
    qb = q.transpose(0, 2, 3, 1, 4).reshape(b, h, 2, nb, Q_BLOCK, d).transpose(3, 0, 1, 2, 4, 5)
    kt = k.transpose(0, 2, 3, 1, 4)
    vt = v.transpose(0, 2, 1, 3)
    key_pos = jnp.arange(s)
    scale = d ** -0.5

    def block(args):
        q_blk, blk = args
        sc = jnp.einsum("bhmqd,bhmkd->bhmqk", q_blk, kt).astype(jnp.float32) * scale
        q_pos = blk * Q_BLOCK + jnp.arange(Q_BLOCK)
        causal = key_pos[None, :] <= q_pos[:, None]
        sc = jnp.where(causal, sc, -jnp.inf)
        p = jax.nn.softmax(sc, axis=-1)
        a = (p[:, :, 0] - lam * p[:, :, 1]).astype(vt.dtype)
        return jnp.einsum("bhqk,bhkd->bhqd", a, vt)

    o = lax.map(block, (qb, jnp.arange(nb)))
    return o.transpose(1, 0, 3, 2, 4).reshape(b, s, h, 2 * d)


def gated_delta_rule(q, k, v, beta, g):
    b, h, s, dk = q.shape
    dv = v.shape[-1]
    n = s // CHUNK
    kb = k * beta[..., None]
    vb = v * beta[..., None]
    ch = lambda t: t.reshape(b, h, n, CHUNK, t.shape[-1])
    q, k, kb, vb = ch(q), ch(k), ch(kb), ch(vb)
    gc = jnp.cumsum(g.reshape(b, h, n, CHUNK), axis=-1)
    tri = jnp.tril(jnp.ones((CHUNK, CHUNK), dtype=bool))
    strict = jnp.tril(jnp.ones((CHUNK, CHUNK), dtype=bool), k=-1)
    diff = gc[..., :, None] - gc[..., None, :]
    decay = jnp.where(tri, jnp.exp(jnp.where(tri, diff, 0.0)), 0.0)
    a_mat = jnp.where(strict, jnp.einsum("bhnid,bhnjd->bhnij", kb, k) * decay, 0.0)
    eye = jnp.eye(CHUNK, dtype=jnp.float32)
    rhs = jnp.concatenate([vb, kb * jnp.exp(gc)[..., None]], axis=-1)
    sol = lax.linalg.triangular_solve(eye + a_mat, rhs, left_side=True, lower=True, unit_diagonal=True)
    u, w = sol[..., :dv], sol[..., dv:]
    qk = jnp.where(tri, jnp.einsum("bhnid,bhnjd->bhnij", q, k) * decay, 0.0)

    def step(state, inp):
        q_i, k_i, u_i, w_i, qk_i, g_i = inp
        v_new = u_i - jnp.einsum("bhcd,bhde->bhce", w_i, state)
        o = (jnp.einsum("bhcd,bhde->bhce", q_i * jnp.exp(g_i)[..., None], state)
             + jnp.einsum("bhcj,bhje->bhce", qk_i, v_new))
        g_last = g_i[..., -1]
        state = (state * jnp.exp(g_last)[..., None, None]
                 + jnp.einsum("bhcd,bhce->bhde", k_i * jnp.exp(g_last[..., None] - g_i)[..., None], v_new))
        return state, o

    mv = lambda t: jnp.moveaxis(t, 2, 0)
    state0 = jnp.zeros((b, h, dk, dv), jnp.float32)
    _, o = lax.scan(step, state0, (mv(q), mv(k), mv(u), mv(w), mv(qk), mv(gc)))
    return jnp.moveaxis(o, 0, 2).reshape(b, h, s, dv)


def setup_inputs(seed: int = 0) -> dict:
    key = jax.random.key(seed)
    ks = jax.random.split(key, 24)
    f32 = jnp.float32
    nrm = lambda k, shape, fan_in: jax.random.normal(k, shape, f32) * fan_in ** -0.5
    gain = lambda k, shape: 1.0 + 0.02 * jax.random.normal(k, shape, f32)
    x = jax.random.normal(ks[0], (BATCH, SEQ, D_MODEL), f32)
    offsets = jax.random.randint(ks[1], (BATCH, 1), 0, 1024, dtype=jnp.int32)
    positions = (jnp.arange(SEQ, dtype=jnp.int32)[None, :] + offsets).astype(jnp.int32)
    return {
        "x": x,
        "positions": positions,
        "norm_mix": gain(ks[2], (DEPTH, D_MODEL)),
        "w_in": nrm(ks[3], (DEPTH, D_MODEL, D_IN), D_MODEL),
        "lambda_q1": LAMBDA_INIT_STD * jax.random.normal(ks[4], (DEPTH, A_HEAD_DIM), f32),
        "lambda_k1": LAMBDA_INIT_STD * jax.random.normal(ks[5], (DEPTH, A_HEAD_DIM), f32),
        "lambda_q2": LAMBDA_INIT_STD * jax.random.normal(ks[6], (DEPTH, A_HEAD_DIM), f32),
        "lambda_k2": LAMBDA_INIT_STD * jax.random.normal(ks[7], (DEPTH, A_HEAD_DIM), f32),
        "a_subln": gain(ks[8], (DEPTH, A_V_DIM)),
        "w_a_out": nrm(ks[9], (DEPTH, A_V_W, D_MODEL), A_V_W),
        "conv_qkv": nrm(ks[10], (DEPTH, CONV_WIDTH, 3 * B_W), CONV_WIDTH),
        "a_log": jnp.log(jax.random.uniform(ks[11], (DEPTH, B_HEADS), f32, 1.0, 16.0)),
        "dt_bias": jnp.log(jnp.expm1(jax.random.uniform(ks[12], (DEPTH, B_HEADS), f32, 1e-3, 0.1))),
        "b_onorm": gain(ks[13], (DEPTH, B_HEAD_DIM)),
        "w_b_out": nrm(ks[14], (DEPTH, B_W, D_MODEL), B_W),
        "w_o": nrm(ks[15], (DEPTH, D_MODEL, D_MODEL), D_MODEL),
        "norm_ffn": gain(ks[16], (DEPTH, D_MODEL)),
        "w_up": nrm(ks[17], (DEPTH, D_MODEL, 2 * D_FF), D_MODEL),
        "ffn_conv": nrm(ks[18], (DEPTH, FFN_CONV_WIDTH, 2 * D_FF), FFN_CONV_WIDTH),
        "ffn_conv_bias": 0.02 * jax.random.normal(ks[19], (DEPTH, 2 * D_FF), f32),
        "w_down": nrm(ks[20], (DEPTH, D_FF, D_MODEL), D_FF),
        "norm_final": gain(ks[21], (D_MODEL,)),
    }


def reference(x, positions, norm_mix, w_in, lambda_q1, lambda_k1, lambda_q2, lambda_k2, a_subln,
              w_a_out, conv_qkv, a_log, dt_bias, b_onorm, w_b_out, w_o, norm_ffn, w_up, ffn_conv,
              ffn_conv_bias, w_down, norm_final):
    b, s, _ = x.shape
    split_points = [int(v) for v in np.cumsum(IN_SIZES)[:-1]]
    for layer in range(DEPTH):
        h = rms_norm(x, norm_mix[layer])
        proj = jnp.einsum("bsd,de->bse", h, w_in[layer])
        qa, ka, va, qb, kb, vb, zb, beta_in, a_in, gate_a, gate_b = jnp.split(proj, split_points, axis=-1)

        qa = rotary(qa.reshape(b, s, A_HEADS * 2, A_HEAD_DIM), positions).reshape(b, s, A_HEADS, 2, A_HEAD_DIM)
        ka = rotary(ka.reshape(b, s, A_HEADS * 2, A_HEAD_DIM), positions).reshape(b, s, A_HEADS, 2, A_HEAD_DIM)
        va = va.reshape(b, s, A_HEADS, A_V_DIM)
        lambda_init = 0.8 - 0.6 * math.exp(-0.3 * layer)
        lam = (jnp.exp(jnp.sum(lambda_q1[layer].astype(jnp.float32) * lambda_k1[layer].astype(jnp.float32)))
               - jnp.exp(jnp.sum(lambda_q2[layer].astype(jnp.float32) * lambda_k2[layer].astype(jnp.float32)))
               + lambda_init)
        oa = diff_attention(qa, ka, va, lam)
        oa = rms_norm(oa, a_subln[layer]) * (1.0 - lambda_init)
        ya = jnp.einsum("bse,ed->bsd", oa.reshape(b, s, A_V_W).astype(x.dtype), w_a_out[layer])

        qkv_b = jax.nn.silu(causal_dwconv(jnp.concatenate([qb, kb, vb], axis=-1), conv_qkv[layer]))
        qb, kb, vb = jnp.split(qkv_b, [B_W, 2 * B_W], axis=-1)
        heads = lambda t: t.reshape(b, s, B_HEADS, B_HEAD_DIM).transpose(0, 2, 1, 3)
        q_d = l2_normalize(heads(qb)) * (B_HEAD_DIM ** -0.5)
        k_d = l2_normalize(heads(kb))
        v_d = heads(vb).astype(jnp.float32)
        beta = jax.nn.sigmoid(beta_in.astype(jnp.float32)).transpose(0, 2, 1)
        g = (-jnp.exp(a_log[layer].astype(jnp.float32))
             * jax.nn.softplus(a_in.astype(jnp.float32) + dt_bias[layer].astype(jnp.float32))).transpose(0, 2, 1)
        ob = gated_delta_rule(q_d, k_d, v_d, beta, g).transpose(0, 2, 1, 3)
        ob = rms_norm(ob, b_onorm[layer]) * jax.nn.silu(zb.reshape(b, s, B_HEADS, B_HEAD_DIM).astype(jnp.float32))
        yb = jnp.einsum("bse,ed->bsd", ob.reshape(b, s, B_W).astype(x.dtype), w_b_out[layer])

        merged = jax.nn.sigmoid(gate_a) * ya + jax.nn.sigmoid(gate_b) * yb
        x = x + jnp.einsum("bsd,de->bse", merged, w_o[layer])

        h2 = rms_norm(x, norm_ffn[layer])
        u = jnp.einsum("bsd,df->bsf", h2, w_up[layer])
        u = causal_dwconv(u, ffn_conv[layer]) + ffn_conv_bias[layer]
        gate, up = jnp.split(u, 2, axis=-1)
        x = x + jnp.einsum("bsf,fd->bsd", jax.nn.silu(gate) * up, w_down[layer])
    return rms_norm(x, norm_final)
```

```python
import functools
import math

import jax
import jax.numpy as jnp
from jax import lax
from jax.experimental import pallas as pl
from jax.experimental.pallas import tpu as pltpu

NORM_EPS = 1e-6
ROPE_THETA = 10000.0
LANES = 128
VMEM_LIMIT = 56 * 1024 * 1024

A_HEADS = 8
A_HEAD_DIM = 64
A_V_DIM = 128
B_HEADS = 8
B_HEAD_DIM = 128
CONV_WIDTH = 4
FFN_CONV_WIDTH = 3
GDN_CHUNK = 64
HALO = 8

F32 = jnp.float32
BF16 = jnp.bfloat16


def _dot(a, b):
    return jnp.dot(a, b, preferred_element_type=F32)


def _const_spec(shape):
    nd = len(shape)
    return pl.BlockSpec(shape, lambda *_: (0,) * nd, pipeline_mode=pl.Buffered(1))


def _sigmoid(x):
    return 1.0 / (1.0 + jnp.exp(-x))


def _silu(x):
    return x * _sigmoid(x)


def _rope_kernel(pos_ref, posr_ref, freq_ref, sign_ref, freqc_ref, cos_ref, sin_ref, cosT_ref, sinT_ref):
    ang = pos_ref[...].astype(F32) * freq_ref[...]
    cos_ref[...] = jnp.cos(ang)
    sin_ref[...] = jnp.sin(ang) * sign_ref[...]
    angT = freqc_ref[...] * posr_ref[...].astype(F32)
    cosT_ref[...] = jnp.cos(angT)
    sinT_ref[...] = jnp.sin(angT)


def _rope_tables(positions, tile):
    n = positions.size
    half = A_HEAD_DIM // 2
    inv_freq = ROPE_THETA ** (-jnp.arange(0, A_HEAD_DIM, 2, dtype=F32) / A_HEAD_DIM)
    freq = jnp.tile(inv_freq, LANES // half).reshape(1, LANES)
    sign = jnp.tile(jnp.concatenate([-jnp.ones((half,), F32), jnp.ones((half,), F32)]),
                    LANES // A_HEAD_DIM).reshape(1, LANES)
    const = lambda shape: pl.BlockSpec(shape, lambda i: (0, 0))
    return pl.pallas_call(
        _rope_kernel,
        out_shape=(jax.ShapeDtypeStruct((n, LANES), F32), jax.ShapeDtypeStruct((n, LANES), F32),
                   jax.ShapeDtypeStruct((half, n), F32), jax.ShapeDtypeStruct((half, n), F32)),
        grid=(n // tile,),
        in_specs=[pl.BlockSpec((tile, 1), lambda i: (i, 0)), pl.BlockSpec((1, tile), lambda i: (0, i)),
                  const((1, LANES)), const((1, LANES)), const((half, 1))],
        out_specs=(pl.BlockSpec((tile, LANES), lambda i: (i, 0)), pl.BlockSpec((tile, LANES), lambda i: (i, 0)),
                   pl.BlockSpec((half, tile), lambda i: (0, i)), pl.BlockSpec((half, tile), lambda i: (0, i))),
        name="rope_tables",
    )(positions.reshape(n, 1), positions.reshape(1, n), freq, sign, inv_freq.reshape(half, 1))


def _rotary(y, cos, sin_signed):
    half = A_HEAD_DIM // 2
    lane = lax.broadcasted_iota(jnp.int32, (1, LANES), 1)
    first = (lane % A_HEAD_DIM) < half
    outs = []
    for b in range(y.shape[1] // LANES):
        blk = y[:, b * LANES:(b + 1) * LANES]
        partner = jnp.where(first, pltpu.roll(blk, LANES - half, axis=1), pltpu.roll(blk, half, axis=1))
        outs.append(blk * cos + partner * sin_signed)
    return jnp.concatenate(outs, axis=1)


def _rotary_t(y, cos_t, sin_t, scale):
    half = A_HEAD_DIM // 2
    outs = []
    for g in range(y.shape[0] // A_HEAD_DIM):
        x1 = y[g * A_HEAD_DIM:g * A_HEAD_DIM + half]
        x2 = y[g * A_HEAD_DIM + half:(g + 1) * A_HEAD_DIM]
        outs.append((x1 * cos_t - x2 * sin_t) * scale)
        outs.append((x2 * cos_t + x1 * sin_t) * scale)
    return jnp.concatenate(outs, axis=0)


def _dot_nt(a, b):
    return lax.dot_general(a, b, (((1,), (1,)), ((), ())), preferred_element_type=F32)


def _inproj_kernel(x_ref, nw_ref, cos_ref, sin_ref, cosT_ref, sinT_ref, wqT_ref, wk_ref, wvT_ref, wpre_ref, wz_ref,
                   wba_ref, wgate_ref, qT_ref, k_ref, vT_ref, pre_ref, z_ref, ba_ref, gate_ref):
    x = x_ref[...]
    h = (x * lax.rsqrt(jnp.mean(x * x, axis=-1, keepdims=True) + NORM_EPS) * nw_ref[...]).astype(BF16)
    qT = _rotary_t(_dot_nt(wqT_ref[...], h), cosT_ref[...], sinT_ref[...], A_HEAD_DIM ** -0.5)
    qT_ref[0, 0] = qT.astype(BF16)
    k_ref[...] = _rotary(_dot(h, wk_ref[...]), cos_ref[...], sin_ref[...]).astype(BF16)
    vT_ref[0, 0] = _dot_nt(wvT_ref[...], h).astype(BF16)
    pre_ref[...] = _dot(h, wpre_ref[...]).astype(BF16)
    z_ref[...] = _dot(h, wz_ref[...]).astype(BF16)
    ba_ref[...] = _dot(h, wba_ref[...])
    gate_ref[...] = _dot(h, wgate_ref[...]).astype(BF16)


def _inproj(x2d, nw, tables, w_in, batch, seq, tile):
    n, d = x2d.shape
    nt = seq // tile
    aw = A_HEADS * 2 * A_HEAD_DIM
    bw = B_HEADS * B_HEAD_DIM
    half = A_HEAD_DIM // 2
    cos, sin, cos_t, sin_t = tables
    wb = w_in.astype(BF16)
    o = 0
    wq_t = wb[:, o:o + aw].T; o += aw
    wk = wb[:, o:o + aw]; o += aw
    wv_t = wb[:, o:o + aw].T; o += aw
    wpre = wb[:, o:o + 3 * bw]; o += 3 * bw
    wz = wb[:, o:o + bw]; o += bw
    wba = jnp.pad(wb[:, o:o + 2 * B_HEADS], ((0, 0), (0, LANES - 2 * B_HEADS))); o += 2 * B_HEADS
    wgate = wb[:, o:o + 2 * d]
    row = lambda w: pl.BlockSpec((tile, w), lambda i: (i, 0))
    col = pl.BlockSpec((half, tile), lambda i: (0, i))
    tspec = pl.BlockSpec((1, 1, aw, tile), lambda i: (i // nt, i % nt, 0, 0))
    return pl.pallas_call(
        _inproj_kernel,
        out_shape=(jax.ShapeDtypeStruct((batch, nt, aw, tile), BF16),
                   jax.ShapeDtypeStruct((n, aw), BF16),
                   jax.ShapeDtypeStruct((batch, nt, aw, tile), BF16),
                   jax.ShapeDtypeStruct((n, 3 * bw), BF16),
                   jax.ShapeDtypeStruct((n, bw), BF16),
                   jax.ShapeDtypeStruct((n, LANES), F32),
                   jax.ShapeDtypeStruct((n, 2 * d), BF16)),
        grid=(n // tile,),
        in_specs=[row(d), _const_spec((1, d)), row(LANES), row(LANES), col, col,
                  _const_spec(wq_t.shape), _const_spec(wk.shape), _const_spec(wv_t.shape),
                  _const_spec(wpre.shape), _const_spec(wz.shape), _const_spec(wba.shape),
                  _const_spec(wgate.shape)],
        out_specs=(tspec, row(aw), tspec, row(3 * bw), row(bw), row(LANES), row(2 * d)),
        compiler_params=pltpu.CompilerParams(dimension_semantics=("arbitrary",), vmem_limit_bytes=VMEM_LIMIT),
        name="inproj",
    )(x2d, nw, cos, sin, cos_t, sin_t, wq_t, wk, wv_t, wpre, wz, wba, wgate)


def _attn_kernel(lam_ref, qT_ref, k_ref, vT_ref, subln_ref, o_ref, qm_ref, m_ref, l_ref, acc_ref, *, tile):
    i = pl.program_id(2)
    dh = A_HEAD_DIM
    qT = qT_ref[0, 0]
    zero = jnp.zeros((dh, tile), BF16)
    qm_ref[:, :tile] = jnp.concatenate([qT[:dh], zero], axis=0)
    qm_ref[:, tile:] = jnp.concatenate([zero, qT[dh:]], axis=0)
    m_ref[...] = jnp.full(m_ref.shape, -jnp.inf, F32)
    l_ref[...] = jnp.zeros(l_ref.shape, F32)
    acc_ref[...] = jnp.zeros(acc_ref.shape, F32)

    def step(j, masked):
        kj = k_ref[0, pl.ds(pl.multiple_of(j * tile, tile), tile), :]
        s = _dot(kj, qm_ref[...])
        if masked:
            kpos = lax.broadcasted_iota(jnp.int32, (tile, 2 * tile), 0)
            qpos = lax.broadcasted_iota(jnp.int32, (tile, 2 * tile), 1) % tile
            s = jnp.where(kpos <= qpos, s, -jnp.inf)
        m_prev = m_ref[...]
        m_new = jnp.maximum(m_prev, jnp.max(s, axis=0, keepdims=True))
        alpha = jnp.exp(m_prev - m_new)
        p = jnp.exp(s - m_new)
        l_ref[...] = alpha * l_ref[...] + jnp.sum(p, axis=0, keepdims=True)
        acc_ref[...] = alpha * acc_ref[...] + _dot(vT_ref[0, j], p.astype(BF16))
        m_ref[...] = m_new

    def body(j, carry):
        step(j, False)
        return carry

    lax.fori_loop(0, i, body, 0)
    step(i, True)

    lam = lam_ref[0, 0]
    inv_l = 1.0 / l_ref[...]
    o = acc_ref[:, :tile] * inv_l[:, :tile] - lam * (acc_ref[:, tile:] * inv_l[:, tile:])
    o = o * lax.rsqrt(jnp.mean(o * o, axis=0, keepdims=True) + NORM_EPS) * subln_ref[...]
    o_ref[...] = o.T.astype(o_ref.dtype)


def _attention(lam, qT, k2d, vT, subln_col, batch, seq, tile):
    nt = seq // tile
    dv = A_V_DIM
    n = batch * seq
    kern = functools.partial(_attn_kernel, tile=tile)
    return pl.pallas_call(
        kern,
        out_shape=jax.ShapeDtypeStruct((n, A_HEADS * dv), BF16),
        grid=(batch, A_HEADS, nt),
        in_specs=[pl.BlockSpec(memory_space=pltpu.SMEM),
                  pl.BlockSpec((1, 1, 2 * A_HEAD_DIM, tile), lambda b, h, i: (b, i, h, 0)),
                  pl.BlockSpec((1, seq, 2 * A_HEAD_DIM), lambda b, h, i: (b, 0, h)),
                  pl.BlockSpec((1, nt, dv, tile), lambda b, h, i: (b, 0, h, 0)),
                  pl.BlockSpec((dv, 1), lambda b, h, i: (0, 0))],
        out_specs=pl.BlockSpec((tile, dv), lambda b, h, i: (b * nt + i, h)),
        scratch_shapes=[pltpu.VMEM((2 * A_HEAD_DIM, 2 * tile), BF16),
                        pltpu.VMEM((1, 2 * tile), F32),
                        pltpu.VMEM((1, 2 * tile), F32),
                        pltpu.VMEM((dv, 2 * tile), F32)],
        compiler_params=pltpu.CompilerParams(dimension_semantics=("arbitrary",) * 3, vmem_limit_bytes=VMEM_LIMIT),
        name="diff_attention",
    )(lam, qT, k2d.reshape(batch, seq, -1), vT, subln_col)


def _merge_kernel(x_ref, oa_ref, ob_ref, gate_ref, wa_ref, wb_ref, wo_ref, o_ref):
    d = x_ref.shape[1]
    ya = _dot(oa_ref[...], wa_ref[...])
    yb = _dot(ob_ref[...], wb_ref[...])
    g = gate_ref[...].astype(F32)
    merged = _sigmoid(g[:, :d]) * ya + _sigmoid(g[:, d:]) * yb
    o_ref[...] = x_ref[...] + _dot(merged.astype(BF16), wo_ref[...])


def _merge(x2d, oa, ob, gates, wa, wb, wo, tile):
    n, d = x2d.shape
    row = lambda w: pl.BlockSpec((tile, w), lambda i: (i, 0))
    return pl.pallas_call(
        _merge_kernel,
        out_shape=jax.ShapeDtypeStruct((n, d), F32),
        grid=(n // tile,),
        in_specs=[row(d), row(oa.shape[1]), row(ob.shape[1]), row(2 * d),
                  _const_spec(wa.shape), _const_spec(wb.shape), _const_spec(wo.shape)],
        out_specs=row(d),
        compiler_params=pltpu.CompilerParams(dimension_semantics=("arbitrary",), vmem_limit_bytes=VMEM_LIMIT),
        name="merge_outproj",
    )(x2d, oa, ob, gates, wa.astype(BF16), wb.astype(BF16), wo.astype(BF16))


def _ffn_kernel(x_ref, nw_ref, wup_ref, cw_ref, cb_ref, wdown_ref, nf_ref, o_ref, h_ref, *, tiles_per_seq, fchunk,
                apply_final):
    i = pl.program_id(0)
    tile, d = x_ref.shape
    dff = wdown_ref.shape[0]

    @pl.when(i % tiles_per_seq == 0)
    def _():
        h_ref[:HALO, :] = jnp.zeros((HALO, d), BF16)

    x = x_ref[...]
    h = (x * lax.rsqrt(jnp.mean(x * x, axis=-1, keepdims=True) + NORM_EPS) * nw_ref[...]).astype(BF16)
    h_ref[HALO:, :] = h
    hext = h_ref[...]
    acc = jnp.zeros((tile, d), F32)
    for c in range(dff // fchunk):
        act = None
        halves = []
        for base in (0, dff):
            lo = base + c * fchunk
            u = _dot(hext, wup_ref[:, lo:lo + fchunk])
            cw = cw_ref[:, lo:lo + fchunk]
            y = cb_ref[:, lo:lo + fchunk] + cw[FFN_CONV_WIDTH - 1:FFN_CONV_WIDTH] * u[HALO:]
            for s in range(1, FFN_CONV_WIDTH):
                y = y + cw[FFN_CONV_WIDTH - 1 - s:FFN_CONV_WIDTH - s] * pltpu.roll(u, s, axis=0)[HALO:]
            halves.append(y)
        act = (_silu(halves[0]) * halves[1]).astype(BF16)
        acc = acc + _dot(act, wdown_ref[c * fchunk:(c + 1) * fchunk, :])
    h_ref[:HALO, :] = h[tile - HALO:, :]
    x2 = x + acc
    if apply_final:
        x2 = x2 * lax.rsqrt(jnp.mean(x2 * x2, axis=-1, keepdims=True) + NORM_EPS) * nf_ref[...]
    o_ref[...] = x2


def _ffn(x2d, nw, wup, cw, cb, wdown, nf, seq, tile, fchunk, apply_final):
    n, d = x2d.shape
    dff = wdown.shape[0]
    row = pl.BlockSpec((tile, d), lambda i: (i, 0))
    kern = functools.partial(_ffn_kernel, tiles_per_seq=seq // tile, fchunk=fchunk, apply_final=apply_final)
    return pl.pallas_call(
        kern,
        out_shape=jax.ShapeDtypeStruct((n, d), F32),
        grid=(n // tile,),
        in_specs=[row, _const_spec((1, d)), _const_spec(wup.shape), _const_spec(cw.shape),
                  _const_spec((1, 2 * dff)), _const_spec(wdown.shape), _const_spec((1, d))],
        out_specs=row,
        scratch_shapes=[pltpu.VMEM((HALO + tile, d), BF16)],
        compiler_params=pltpu.CompilerParams(dimension_semantics=("arbitrary",), vmem_limit_bytes=VMEM_LIMIT),
        name="convffn",
    )(x2d, nw, wup.astype(BF16), cw, cb, wdown.astype(BF16), nf)


def _gdn_kernel(pre_ref, z_ref, ba_ref, baT_ref, cw_ref, alog_ref, dtb_ref, alogc_ref, dtbc_ref, onorm_ref,
                o_ref, carry_ref, state_ref, *, steps_per_seq):
    c = GDN_CHUNK
    dk = B_HEAD_DIM
    bw = B_HEADS * dk
    step_id = pl.program_id(0)

    @pl.when(step_id % steps_per_seq == 0)
    def _():
        carry_ref[...] = jnp.zeros(carry_ref.shape, F32)
        state_ref[...] = jnp.zeros(state_ref.shape, F32)

    cur = pre_ref[...].astype(F32)
    ext = jnp.concatenate([carry_ref[...], cur], axis=0)
    cw = cw_ref[...]
    y = cw[CONV_WIDTH - 1:CONV_WIDTH] * cur
    for s in range(1, CONV_WIDTH):
        y = y + cw[CONV_WIDTH - 1 - s:CONV_WIDTH - s] * pltpu.roll(ext, s, axis=0)[HALO:]
    carry_ref[...] = cur[c - HALO:, :]
    qkv = _silu(y)

    ba = ba_ref[...]
    baT = baT_ref[0]
    beta = _sigmoid(ba)
    softplus = lambda t: jnp.maximum(t, 0.0) + jnp.log1p(jnp.exp(-jnp.abs(t)))
    g = -jnp.exp(alog_ref[...]) * softplus(ba + dtb_ref[...])
    gT = -jnp.exp(alogc_ref[...]) * softplus(baT + dtbc_ref[...])
    r = lax.broadcasted_iota(jnp.int32, (c, c), 0)
    q_ = lax.broadcasted_iota(jnp.int32, (c, c), 1)
    lower = r >= q_
    strict = r > q_
    gc = jnp.dot(lower.astype(F32), g, precision=lax.Precision.HIGHEST, preferred_element_type=F32)
    gcT = jnp.dot(gT, (r <= q_).astype(F32), precision=lax.Precision.HIGHEST, preferred_element_type=F32)
    eye = (r == q_).astype(F32)

    z = z_ref[...].astype(F32)
    outs = []
    for h in range(B_HEADS):
        sl = slice(h * dk, (h + 1) * dk)
        qh = qkv[:, h * dk:(h + 1) * dk]
        kh = qkv[:, bw + h * dk:bw + (h + 1) * dk]
        vh = qkv[:, 2 * bw + h * dk:2 * bw + (h + 1) * dk]
        qh = qh * lax.rsqrt(jnp.sum(qh * qh, axis=-1, keepdims=True) + NORM_EPS) * (dk ** -0.5)
        kh = kh * lax.rsqrt(jnp.sum(kh * kh, axis=-1, keepdims=True) + NORM_EPS)
        gl = B_HEADS + h
        bh = beta[:, h:h + 1]
        gch = gc[:, gl:gl + 1]
        gcr = gcT[gl:gl + 1, :]
        glast = gc[c - 1:c, gl:gl + 1]
        diff = gch - gcr
        decay = jnp.where(lower, jnp.exp(jnp.where(lower, diff, 0.0)), 0.0)
        kb = kh * bh
        vb = vh * bh
        kbf = kb.astype(BF16)
        khf = kh.astype(BF16)
        nt = (((1,), (1,)), ((), ()))
        a_mat = jnp.where(strict, lax.dot_general(kbf, khf, nt, preferred_element_type=F32) * decay, 0.0)
        qk = jnp.where(lower, lax.dot_general(qh.astype(BF16), khf, nt, preferred_element_type=F32) * decay, 0.0)
        t_inv = eye - a_mat
        pw = a_mat
        n_sq = int(math.log2(c)) - 1
        for _ in range(n_sq):
            pw = _dot(pw.astype(BF16), pw.astype(BF16))
            t_inv = t_inv + _dot(t_inv.astype(BF16), pw.astype(BF16))
        rhs = jnp.concatenate([vb, kb * jnp.exp(gch)], axis=1).astype(BF16)
        sol = _dot(t_inv.astype(BF16), rhs)
        u = sol[:, :dk]
        w = sol[:, dk:]
        state = state_ref[h]
        sb = state.astype(BF16)
        v_new = u - _dot(w.astype(BF16), sb)
        o = _dot((qh * jnp.exp(gch)).astype(BF16), sb) + _dot(qk.astype(BF16), v_new.astype(BF16))
        kdec = (kh * jnp.exp(glast - gch)).astype(BF16)
        tn = (((0,), (0,)), ((), ()))
        state_ref[h] = state * jnp.exp(glast) + lax.dot_general(kdec, v_new.astype(BF16), tn,
                                                                preferred_element_type=F32)
        o = o * lax.rsqrt(jnp.mean(o * o, axis=-1, keepdims=True) + NORM_EPS) * onorm_ref[...]
        outs.append(o * _silu(z[:, sl]))
    o_ref[...] = jnp.concatenate(outs, axis=1).astype(o_ref.dtype)


def _gdn(pre, z, ba, conv_w, a_log, dt_bias, onorm, seq):
    n = pre.shape[0]
    c = GDN_CHUNK
    bw = B_HEADS * B_HEAD_DIM
    baT = ba.reshape(n // c, c, LANES).transpose(0, 2, 1)
    pad = lambda v: jnp.pad(v.astype(F32), (B_HEADS, LANES - 2 * B_HEADS))
    alog, dtb = pad(a_log), pad(dt_bias)
    row = lambda w: pl.BlockSpec((c, w), lambda i: (i, 0))
    kern = functools.partial(_gdn_kernel, steps_per_seq=seq // c)
    return pl.pallas_call(
        kern,
        out_shape=jax.ShapeDtypeStruct((n, bw), BF16),
        grid=(n // c,),
        in_specs=[row(3 * bw), row(bw), row(LANES), pl.BlockSpec((1, LANES, c), lambda i: (i, 0, 0)),
                  _const_spec(conv_w.shape), _const_spec((1, LANES)), _const_spec((1, LANES)),
                  _const_spec((LANES, 1)), _const_spec((LANES, 1)), _const_spec((1, B_HEAD_DIM))],
        out_specs=row(bw),
        scratch_shapes=[pltpu.VMEM((HALO, 3 * bw), F32),
                        pltpu.VMEM((B_HEADS, B_HEAD_DIM, B_HEAD_DIM), F32)],
        compiler_params=pltpu.CompilerParams(dimension_semantics=("arbitrary",), vmem_limit_bytes=VMEM_LIMIT),
        name="gated_deltanet",
    )(pre, z, ba, baT, conv_w, alog.reshape(1, -1), dtb.reshape(1, -1), alog.reshape(-1, 1),
      dtb.reshape(-1, 1), onorm.reshape(1, -1))


def _lam_kernel(q1_ref, k1_ref, q2_ref, k2_ref, o_ref, *, lambda_init):
    s1 = jnp.sum(q1_ref[...] * k1_ref[...], axis=-1, keepdims=True)
    s2 = jnp.sum(q2_ref[...] * k2_ref[...], axis=-1, keepdims=True)
    o_ref[...] = jnp.exp(s1) - jnp.exp(s2) + lambda_init


def _lambda(q1, k1, q2, k2, lambda_init):
    spec = pl.BlockSpec((1, q1.shape[-1]), lambda: (0, 0))
    return pl.pallas_call(
        functools.partial(_lam_kernel, lambda_init=lambda_init),
        out_shape=jax.ShapeDtypeStruct((1, 1), F32),
        in_specs=[spec] * 4,
        out_specs=pl.BlockSpec((1, 1), lambda: (0, 0)),
        name="lambda",
    )(q1.reshape(1, -1), k1.reshape(1, -1), q2.reshape(1, -1), k2.reshape(1, -1))


def _pick_tile(seq, pref):
    t = min(seq, pref)
    assert seq % t == 0
    return t


def kernel(x, positions, norm_mix, w_in, lambda_q1, lambda_k1, lambda_q2, lambda_k2, a_subln, w_a_out, conv_qkv,
           a_log, dt_bias, b_onorm, w_b_out, w_o, norm_ffn, w_up, ffn_conv, ffn_conv_bias, w_down, norm_final):
    batch, seq, d = x.shape
    depth = w_in.shape[0]
    tile = _pick_tile(seq, 512)
    x2d = x.reshape(batch * seq, d)
    tables = _rope_tables(positions, tile)
    for layer in range(depth):
        lambda_init = 0.8 - 0.6 * math.exp(-0.3 * layer)
        qT, k2d, vT, pre, z, ba, gates = _inproj(x2d, norm_mix[layer].reshape(1, d), tables, w_in[layer],
                                                 batch, seq, tile)
        lam = _lambda(lambda_q1[layer], lambda_k1[layer], lambda_q2[layer], lambda_k2[layer], lambda_init)
        subln = (a_subln[layer] * (1.0 - lambda_init)).reshape(-1, 1)
        oa = _attention(lam, qT, k2d, vT, subln, batch, seq, tile)
        ob = _gdn(pre, z, ba, conv_qkv[layer], a_log[layer], dt_bias[layer], b_onorm[layer], seq)
        x2d = _merge(x2d, oa, ob, gates, w_a_out[layer], w_b_out[layer], w_o[layer], tile)
        dff = w_down.shape[1]
        x2d = _ffn(x2d, norm_ffn[layer].reshape(1, d), w_up[layer], ffn_conv[layer],
                   ffn_conv_bias[layer].reshape(1, -1), w_down[layer], norm_final.reshape(1, d), seq, tile,
                   fchunk=256 if dff % 256 == 0 else LANES, apply_final=layer == depth - 1)
    return x2d.reshape(batch, seq, d)
```

```python
import functools
import math

import jax
import jax.numpy as jnp
from jax import lax
from jax.experimental import pallas as pl
from jax.experimental.pallas import tpu as pltpu

NORM_EPS = 1e-6
ROPE_THETA = 10000.0
LANES = 128
MXU_COLS = 256
LOG2E = math.log2(math.e)
VMEM_LIMIT = 56 * 1024 * 1024

A_HEADS = 8
A_HEAD_DIM = 64
A_V_DIM = 128
B_HEADS = 8
B_HEAD_DIM = 128
CONV_WIDTH = 4
FFN_CONV_WIDTH = 3
GDN_CHUNK = 64
HALO = 8

F32 = jnp.float32
BF16 = jnp.bfloat16


def _dot(a, b):
    return jnp.dot(a, b, preferred_element_type=F32)


def _const_spec(shape):
    nd = len(shape)
    return pl.BlockSpec(shape, lambda *_: (0,) * nd, pipeline_mode=pl.Buffered(1))


def _sigmoid(x):
    return 1.0 / (1.0 + jnp.exp(-x))


def _silu(x):
    return x * _sigmoid(x)


def _rope_kernel(pos_ref, posr_ref, freq_ref, sign_ref, freqc_ref, cos_ref, sin_ref, cosT_ref, sinT_ref):
    ang = pos_ref[...].astype(F32) * freq_ref[...]
    cos_ref[...] = jnp.cos(ang)
    sin_ref[...] = jnp.sin(ang) * sign_ref[...]
    angT = freqc_ref[...] * posr_ref[...].astype(F32)
    cosT_ref[...] = jnp.cos(angT)
    sinT_ref[...] = jnp.sin(angT)


def _rope_tables(positions, tile):
    n = positions.size
    half = A_HEAD_DIM // 2
    inv_freq = ROPE_THETA ** (-jnp.arange(0, A_HEAD_DIM, 2, dtype=F32) / A_HEAD_DIM)
    freq = jnp.tile(inv_freq, LANES // half).reshape(1, LANES)
    sign = jnp.tile(jnp.concatenate([-jnp.ones((half,), F32), jnp.ones((half,), F32)]),
                    LANES // A_HEAD_DIM).reshape(1, LANES)
    const = lambda shape: pl.BlockSpec(shape, lambda i: (0, 0))
    return pl.pallas_call(
        _rope_kernel,
        out_shape=(jax.ShapeDtypeStruct((n, LANES), F32), jax.ShapeDtypeStruct((n, LANES), F32),
                   jax.ShapeDtypeStruct((half, n), F32), jax.ShapeDtypeStruct((half, n), F32)),
        grid=(n // tile,),
        in_specs=[pl.BlockSpec((tile, 1), lambda i: (i, 0)), pl.BlockSpec((1, tile), lambda i: (0, i)),
                  const((1, LANES)), const((1, LANES)), const((half, 1))],
        out_specs=(pl.BlockSpec((tile, LANES), lambda i: (i, 0)), pl.BlockSpec((tile, LANES), lambda i: (i, 0)),
                   pl.BlockSpec((half, tile), lambda i: (0, i)), pl.BlockSpec((half, tile), lambda i: (0, i))),
        name="rope_tables",
    )(positions.reshape(n, 1), positions.reshape(1, n), freq, sign, inv_freq.reshape(half, 1))


def _rotary(y, cos, sin_signed):
    half = A_HEAD_DIM // 2
    lane = lax.broadcasted_iota(jnp.int32, (1, LANES), 1)
    first = (lane % A_HEAD_DIM) < half
    outs = []
    for b in range(y.shape[1] // LANES):
        blk = y[:, b * LANES:(b + 1) * LANES]
        partner = jnp.where(first, pltpu.roll(blk, LANES - half, axis=1), pltpu.roll(blk, half, axis=1))
        outs.append(blk * cos + partner * sin_signed)
    return jnp.concatenate(outs, axis=1)


def _rotary_t(y, cos_t, sin_t, scale):
    half = A_HEAD_DIM // 2
    outs = []
    for g in range(y.shape[0] // A_HEAD_DIM):
        x1 = y[g * A_HEAD_DIM:g * A_HEAD_DIM + half]
        x2 = y[g * A_HEAD_DIM + half:(g + 1) * A_HEAD_DIM]
        outs.append((x1 * cos_t - x2 * sin_t) * scale)
        outs.append((x2 * cos_t + x1 * sin_t) * scale)
    return jnp.concatenate(outs, axis=0)


def _dot_nt(a, b):
    return lax.dot_general(a, b, (((1,), (1,)), ((), ())), preferred_element_type=F32)


def _inproj_kernel(x_ref, nw_ref, cos_ref, sin_ref, cosT_ref, sinT_ref, wqT_ref, wk_ref, wvT_ref, wpre_ref, wz_ref,
                   wba_ref, wgate_ref, qT_ref, k_ref, vT_ref, pre_ref, z_ref, ba_ref, gate_ref):
    x = x_ref[...]
    h = (x * lax.rsqrt(jnp.mean(x * x, axis=-1, keepdims=True) + NORM_EPS) * nw_ref[...]).astype(BF16)
    qT = _rotary_t(_dot_nt(wqT_ref[...], h), cosT_ref[...], sinT_ref[...], A_HEAD_DIM ** -0.5 * LOG2E)
    qT_ref[0, 0] = qT.astype(BF16)
    k_ref[...] = _rotary(_dot(h, wk_ref[...]), cos_ref[...], sin_ref[...]).astype(BF16)
    vT_ref[0, 0] = _dot_nt(wvT_ref[...], h).astype(BF16)
    pre_ref[...] = _dot(h, wpre_ref[...]).astype(BF16)
    z_ref[...] = _dot(h, wz_ref[...]).astype(BF16)
    ba_ref[...] = _dot(h, wba_ref[...])
    gate_ref[...] = _dot(h, wgate_ref[...]).astype(BF16)


def _inproj(x2d, nw, tables, w_in, batch, seq, tile):
    n, d = x2d.shape
    nt = seq // tile
    aw = A_HEADS * 2 * A_HEAD_DIM
    bw = B_HEADS * B_HEAD_DIM
    half = A_HEAD_DIM // 2
    cos, sin, cos_t, sin_t = tables
    wb = w_in.astype(BF16)
    o = 0
    wq_t = wb[:, o:o + aw].T; o += aw
    wk = wb[:, o:o + aw]; o += aw
    wv_t = wb[:, o:o + aw].T; o += aw
    wpre = wb[:, o:o + 3 * bw]; o += 3 * bw
    wz = wb[:, o:o + bw]; o += bw
    wba = jnp.pad(wb[:, o:o + 2 * B_HEADS], ((0, 0), (0, LANES - 2 * B_HEADS))); o += 2 * B_HEADS
    wgate = wb[:, o:o + 2 * d]
    row = lambda w: pl.BlockSpec((tile, w), lambda i: (i, 0))
    col = pl.BlockSpec((half, tile), lambda i: (0, i))
    tspec = pl.BlockSpec((1, 1, aw, tile), lambda i: (i // nt, i % nt, 0, 0))
    return pl.pallas_call(
        _inproj_kernel,
        out_shape=(jax.ShapeDtypeStruct((batch, nt, aw, tile), BF16),
                   jax.ShapeDtypeStruct((n, aw), BF16),
                   jax.ShapeDtypeStruct((batch, nt, aw, tile), BF16),
                   jax.ShapeDtypeStruct((n, 3 * bw), BF16),
                   jax.ShapeDtypeStruct((n, bw), BF16),
                   jax.ShapeDtypeStruct((n, LANES), F32),
                   jax.ShapeDtypeStruct((n, 2 * d), BF16)),
        grid=(n // tile,),
        in_specs=[row(d), _const_spec((1, d)), row(LANES), row(LANES), col, col,
                  _const_spec(wq_t.shape), _const_spec(wk.shape), _const_spec(wv_t.shape),
                  _const_spec(wpre.shape), _const_spec(wz.shape), _const_spec(wba.shape),
                  _const_spec(wgate.shape)],
        out_specs=(tspec, row(aw), tspec, row(3 * bw), row(bw), row(LANES), row(2 * d)),
        compiler_params=pltpu.CompilerParams(dimension_semantics=("arbitrary",), vmem_limit_bytes=VMEM_LIMIT),
        name="inproj",
    )(x2d, nw, cos, sin, cos_t, sin_t, wq_t, wk, wv_t, wpre, wz, wba, wgate)


def _attn_kernel(lam_ref, qT_ref, k_ref, vT_ref, subln_ref, o_ref, qm_ref, m_ref, l_ref, acc_ref, *, tile, cb):
    i = pl.program_id(2)
    dh = A_HEAD_DIM
    qT = qT_ref[0, 0]
    zero = jnp.zeros((dh, tile), BF16)
    qm_ref[:, :tile] = jnp.concatenate([qT[:dh], zero], axis=0)
    qm_ref[:, tile:] = jnp.concatenate([zero, qT[dh:]], axis=0)
    m_ref[...] = jnp.full(m_ref.shape, -jnp.inf, F32)
    l_ref[...] = jnp.zeros(l_ref.shape, F32)
    acc_ref[...] = jnp.zeros(acc_ref.shape, F32)

    ncb = 2 * tile // cb

    def step(j, masked):
        kj = k_ref[0, pl.ds(pl.multiple_of(j * tile, tile), tile), :]
        vj = vT_ref[0, j]
        s_next = _dot(kj, qm_ref[:, :cb])
        for c in range(ncb):
            sl = slice(c * cb, (c + 1) * cb)
            s = s_next
            if c + 1 < ncb:
                s_next = _dot(kj, qm_ref[:, (c + 1) * cb:(c + 2) * cb])
            if masked:
                kpos = lax.broadcasted_iota(jnp.int32, (tile, cb), 0)
                qpos = lax.broadcasted_iota(jnp.int32, (tile, cb), 1) + (c * cb) % tile
                s = jnp.where(kpos <= qpos, s, -jnp.inf)
            m_prev = m_ref[:, sl]
            m_new = jnp.maximum(m_prev, jnp.max(s, axis=0, keepdims=True))
            alpha = jnp.exp2(m_prev - m_new)
            p = jnp.exp2(s - m_new)
            l_ref[:, sl] = alpha * l_ref[:, sl] + jnp.sum(p, axis=0, keepdims=True)
            acc_ref[:, sl] = alpha * acc_ref[:, sl] + _dot(vj, p.astype(BF16))
            m_ref[:, sl] = m_new

    def body(j, carry):
        step(j, False)
        return carry

    lax.fori_loop(0, i, body, 0)
    step(i, True)

    lam = lam_ref[0, 0]
    inv_l = 1.0 / l_ref[...]
    o = acc_ref[:, :tile] * inv_l[:, :tile] - lam * (acc_ref[:, tile:] * inv_l[:, tile:])
    o = o * lax.rsqrt(jnp.mean(o * o, axis=0, keepdims=True) + NORM_EPS) * subln_ref[...]
    o_ref[...] = o.T.astype(o_ref.dtype)


def _attention(lam, qT, k2d, vT, subln_col, batch, seq, tile):
    nt = seq // tile
    dv = A_V_DIM
    n = batch * seq
    kern = functools.partial(_attn_kernel, tile=tile, cb=min(tile, MXU_COLS))
    return pl.pallas_call(
        kern,
        out_shape=jax.ShapeDtypeStruct((n, A_HEADS * dv), BF16),
        grid=(batch, A_HEADS, nt),
        in_specs=[pl.BlockSpec(memory_space=pltpu.SMEM),
                  pl.BlockSpec((1, 1, 2 * A_HEAD_DIM, tile), lambda b, h, i: (b, i, h, 0)),
                  pl.BlockSpec((1, seq, 2 * A_HEAD_DIM), lambda b, h, i: (b, 0, h)),
                  pl.BlockSpec((1, nt, dv, tile), lambda b, h, i: (b, 0, h, 0)),
                  pl.BlockSpec((dv, 1), lambda b, h, i: (0, 0))],
        out_specs=pl.BlockSpec((tile, dv), lambda b, h, i: (b * nt + i, h)),
        scratch_shapes=[pltpu.VMEM((2 * A_HEAD_DIM, 2 * tile), BF16),
                        pltpu.VMEM((1, 2 * tile), F32),
                        pltpu.VMEM((1, 2 * tile), F32),
                        pltpu.VMEM((dv, 2 * tile), F32)],
        compiler_params=pltpu.CompilerParams(dimension_semantics=("arbitrary",) * 3, vmem_limit_bytes=VMEM_LIMIT),
        name="diff_attention",
    )(lam, qT, k2d.reshape(batch, seq, -1), vT, subln_col)


def _merge_kernel(x_ref, oa_ref, ob_ref, gate_ref, wa_ref, wb_ref, wo_ref, o_ref):
    d = x_ref.shape[1]
    ya = _dot(oa_ref[...], wa_ref[...])
    yb = _dot(ob_ref[...], wb_ref[...])
    g = gate_ref[...].astype(F32)
    merged = _sigmoid(g[:, :d]) * ya + _sigmoid(g[:, d:]) * yb
    o_ref[...] = x_ref[...] + _dot(merged.astype(BF16), wo_ref[...])


def _merge(x2d, oa, ob, gates, wa, wb, wo, tile):
    n, d = x2d.shape
    row = lambda w: pl.BlockSpec((tile, w), lambda i: (i, 0))
    return pl.pallas_call(
        _merge_kernel,
        out_shape=jax.ShapeDtypeStruct((n, d), F32),
        grid=(n // tile,),
        in_specs=[row(d), row(oa.shape[1]), row(ob.shape[1]), row(2 * d),
                  _const_spec(wa.shape), _const_spec(wb.shape), _const_spec(wo.shape)],
        out_specs=row(d),
        compiler_params=pltpu.CompilerParams(dimension_semantics=("arbitrary",), vmem_limit_bytes=VMEM_LIMIT),
        name="merge_outproj",
    )(x2d, oa, ob, gates, wa.astype(BF16), wb.astype(BF16), wo.astype(BF16))


def _ffn_kernel(x_ref, nw_ref, wup_ref, cw_ref, cb_ref, wdown_ref, nf_ref, o_ref, h_ref, *, tiles_per_seq, fchunk,
                apply_final):
    i = pl.program_id(0)
    tile, d = x_ref.shape
    dff = wdown_ref.shape[0]

    @pl.when(i % tiles_per_seq == 0)
    def _():
        h_ref[:HALO, :] = jnp.zeros((HALO, d), BF16)

    x = x_ref[...]
    h = (x * lax.rsqrt(jnp.mean(x * x, axis=-1, keepdims=True) + NORM_EPS) * nw_ref[...]).astype(BF16)
    h_ref[HALO:, :] = h
    hext = h_ref[...]
    acc = jnp.zeros((tile, d), F32)
    for c in range(dff // fchunk):
        act = None
        halves = []
        for base in (0, dff):
            lo = base + c * fchunk
            u = _dot(hext, wup_ref[:, lo:lo + fchunk])
            cw = cw_ref[:, lo:lo + fchunk]
            y = cb_ref[:, lo:lo + fchunk] + cw[FFN_CONV_WIDTH - 1:FFN_CONV_WIDTH] * u[HALO:]
            for s in range(1, FFN_CONV_WIDTH):
                y = y + cw[FFN_CONV_WIDTH - 1 - s:FFN_CONV_WIDTH - s] * pltpu.roll(u, s, axis=0)[HALO:]
            halves.append(y)
        act = (_silu(halves[0]) * halves[1]).astype(BF16)
        acc = acc + _dot(act, wdown_ref[c * fchunk:(c + 1) * fchunk, :])
    h_ref[:HALO, :] = h[tile - HALO:, :]
    x2 = x + acc
    if apply_final:
        x2 = x2 * lax.rsqrt(jnp.mean(x2 * x2, axis=-1, keepdims=True) + NORM_EPS) * nf_ref[...]
    o_ref[...] = x2


def _ffn(x2d, nw, wup, cw, cb, wdown, nf, seq, tile, fchunk, apply_final):
    n, d = x2d.shape
    dff = wdown.shape[0]
    row = pl.BlockSpec((tile, d), lambda i: (i, 0))
    kern = functools.partial(_ffn_kernel, tiles_per_seq=seq // tile, fchunk=fchunk, apply_final=apply_final)
    return pl.pallas_call(
        kern,
        out_shape=jax.ShapeDtypeStruct((n, d), F32),
        grid=(n // tile,),
        in_specs=[row, _const_spec((1, d)), _const_spec(wup.shape), _const_spec(cw.shape),
                  _const_spec((1, 2 * dff)), _const_spec(wdown.shape), _const_spec((1, d))],
        out_specs=row,
        scratch_shapes=[pltpu.VMEM((HALO + tile, d), BF16)],
        compiler_params=pltpu.CompilerParams(dimension_semantics=("arbitrary",), vmem_limit_bytes=VMEM_LIMIT),
        name="convffn",
    )(x2d, nw, wup.astype(BF16), cw, cb, wdown.astype(BF16), nf)


def _gdn_kernel(pre_ref, z_ref, ba_ref, baT_ref, cw_ref, alog_ref, dtb_ref, alogc_ref, dtbc_ref, onorm_ref,
                o_ref, carry_ref, state_ref, *, steps_per_seq):
    c = GDN_CHUNK
    dk = B_HEAD_DIM
    bw = B_HEADS * dk
    step_id = pl.program_id(0)

    @pl.when(step_id % steps_per_seq == 0)
    def _():
        carry_ref[...] = jnp.zeros(carry_ref.shape, F32)
        state_ref[...] = jnp.zeros(state_ref.shape, F32)

    cur = pre_ref[...].astype(F32)
    ext = jnp.concatenate([carry_ref[...], cur], axis=0)
    cw = cw_ref[...]
    y = cw[CONV_WIDTH - 1:CONV_WIDTH] * cur
    for s in range(1, CONV_WIDTH):
        y = y + cw[CONV_WIDTH - 1 - s:CONV_WIDTH - s] * pltpu.roll(ext, s, axis=0)[HALO:]
    carry_ref[...] = cur[c - HALO:, :]
    qkv = _silu(y)

    ba = ba_ref[...]
    baT = baT_ref[0]
    beta = _sigmoid(ba)
    softplus = lambda t: jnp.maximum(t, 0.0) + jnp.log1p(jnp.exp(-jnp.abs(t)))
    g = -jnp.exp(alog_ref[...]) * softplus(ba + dtb_ref[...])
    gT = -jnp.exp(alogc_ref[...]) * softplus(baT + dtbc_ref[...])
    r = lax.broadcasted_iota(jnp.int32, (c, c), 0)
    q_ = lax.broadcasted_iota(jnp.int32, (c, c), 1)
    lower = r >= q_
    strict = r > q_
    gc = jnp.dot(lower.astype(F32), g, precision=lax.Precision.HIGHEST, preferred_element_type=F32)
    gcT = jnp.dot(gT, (r <= q_).astype(F32), precision=lax.Precision.HIGHEST, preferred_element_type=F32)

    heads = range(B_HEADS)
    nt = (((1,), (1,)), ((), ()))
    tn = (((0,), (0,)), ((), ()))
    qs, ks, vs, bhs, gchs, glasts, decays = [], [], [], [], [], [], []
    for h in heads:
        qh = qkv[:, h * dk:(h + 1) * dk]
        kh = qkv[:, bw + h * dk:bw + (h + 1) * dk]
        qs.append(qh * lax.rsqrt(jnp.sum(qh * qh, axis=-1, keepdims=True) + NORM_EPS) * (dk ** -0.5))
        ks.append(kh * lax.rsqrt(jnp.sum(kh * kh, axis=-1, keepdims=True) + NORM_EPS))
        vs.append(qkv[:, 2 * bw + h * dk:2 * bw + (h + 1) * dk])
        gl = B_HEADS + h
        bhs.append(beta[:, h:h + 1])
        gchs.append(gc[:, gl:gl + 1])
        glasts.append(gc[c - 1:c, gl:gl + 1])
        diff = gchs[h] - gcT[gl:gl + 1, :]
        decays.append(jnp.where(lower, jnp.exp(jnp.where(lower, diff, 0.0)), 0.0))
    kbs = [ks[h] * bhs[h] for h in heads]
    kfs = [ks[h].astype(BF16) for h in heads]
    kq = [lax.dot_general(jnp.concatenate([kbs[h], qs[h]], axis=0).astype(BF16), kfs[h], nt,
                          preferred_element_type=F32) for h in heads]
    a_mats = [jnp.where(strict, kq[h][:c] * decays[h], 0.0) for h in heads]
    qks = [jnp.where(lower, kq[h][c:] * decays[h], 0.0).astype(BF16) for h in heads]
    xs = [jnp.concatenate([vs[h] * bhs[h], kbs[h] * jnp.exp(gchs[h])], axis=1) for h in heads]
    pws = [a.astype(BF16) for a in a_mats]
    xs = [xs[h] - _dot(pws[h], xs[h].astype(BF16)) for h in heads]
    for _ in range(int(math.log2(c)) - 1):
        pws = [_dot(p, p).astype(BF16) for p in pws]
        xs = [xs[h] + _dot(pws[h], xs[h].astype(BF16)) for h in heads]
    states = [state_ref[h] for h in heads]
    wq = [jnp.concatenate([xs[h][:, dk:], qs[h] * jnp.exp(gchs[h])], axis=0).astype(BF16) for h in heads]
    ws = [_dot(wq[h], states[h].astype(BF16)) for h in heads]
    v_new = [(xs[h][:, :dk] - ws[h][:c]).astype(BF16) for h in heads]
    kdec = [(ks[h] * jnp.exp(glasts[h] - gchs[h])).astype(BF16) for h in heads]
    for h in heads:
        state_ref[h] = states[h] * jnp.exp(glasts[h]) + lax.dot_general(kdec[h], v_new[h], tn,
                                                                        preferred_element_type=F32)
    outs = [ws[h][c:] + _dot(qks[h], v_new[h]) for h in heads]
    z = z_ref[...].astype(F32)
    onorm = onorm_ref[...]
    for h in heads:
        o = outs[h]
        o = o * lax.rsqrt(jnp.mean(o * o, axis=-1, keepdims=True) + NORM_EPS) * onorm
        outs[h] = o * _silu(z[:, h * dk:(h + 1) * dk])
    o_ref[...] = jnp.concatenate(outs, axis=1).astype(o_ref.dtype)


def _gdn(pre, z, ba, conv_w, a_log, dt_bias, onorm, seq):
    n = pre.shape[0]
    c = GDN_CHUNK
    bw = B_HEADS * B_HEAD_DIM
    baT = ba.reshape(n // c, c, LANES).transpose(0, 2, 1)
    pad = lambda v: jnp.pad(v.astype(F32), (B_HEADS, LANES - 2 * B_HEADS))
    alog, dtb = pad(a_log), pad(dt_bias)
    row = lambda w: pl.BlockSpec((c, w), lambda i: (i, 0))
    kern = functools.partial(_gdn_kernel, steps_per_seq=seq // c)
    return pl.pallas_call(
        kern,
        out_shape=jax.ShapeDtypeStruct((n, bw), BF16),
        grid=(n // c,),
        in_specs=[row(3 * bw), row(bw), row(LANES), pl.BlockSpec((1, LANES, c), lambda i: (i, 0, 0)),
                  _const_spec(conv_w.shape), _const_spec((1, LANES)), _const_spec((1, LANES)),
                  _const_spec((LANES, 1)), _const_spec((LANES, 1)), _const_spec((1, B_HEAD_DIM))],
        out_specs=row(bw),
        scratch_shapes=[pltpu.VMEM((HALO, 3 * bw), F32),
                        pltpu.VMEM((B_HEADS, B_HEAD_DIM, B_HEAD_DIM), F32)],
        compiler_params=pltpu.CompilerParams(dimension_semantics=("arbitrary",), vmem_limit_bytes=VMEM_LIMIT),
        name="gated_deltanet",
    )(pre, z, ba, baT, conv_w, alog.reshape(1, -1), dtb.reshape(1, -1), alog.reshape(-1, 1),
      dtb.reshape(-1, 1), onorm.reshape(1, -1))


def _lam_kernel(q1_ref, k1_ref, q2_ref, k2_ref, o_ref, *, lambda_init):
    s1 = jnp.sum(q1_ref[...] * k1_ref[...], axis=-1, keepdims=True)
    s2 = jnp.sum(q2_ref[...] * k2_ref[...], axis=-1, keepdims=True)
    o_ref[...] = jnp.exp(s1) - jnp.exp(s2) + lambda_init


def _lambda(q1, k1, q2, k2, lambda_init):
    spec = pl.BlockSpec((1, q1.shape[-1]), lambda: (0, 0))
    return pl.pallas_call(
        functools.partial(_lam_kernel, lambda_init=lambda_init),
        out_shape=jax.ShapeDtypeStruct((1, 1), F32),
        in_specs=[spec] * 4,
        out_specs=pl.BlockSpec((1, 1), lambda: (0, 0)),
        name="lambda",
    )(q1.reshape(1, -1), k1.reshape(1, -1), q2.reshape(1, -1), k2.reshape(1, -1))


def _pick_tile(seq, pref):
    t = min(seq, pref)
    assert seq % t == 0
    return t


def kernel(x, positions, norm_mix, w_in, lambda_q1, lambda_k1, lambda_q2, lambda_k2, a_subln, w_a_out, conv_qkv,
           a_log, dt_bias, b_onorm, w_b_out, w_o, norm_ffn, w_up, ffn_conv, ffn_conv_bias, w_down, norm_final):
    batch, seq, d = x.shape
    depth = w_in.shape[0]
    tile = _pick_tile(seq, 512)
    x2d = x.reshape(batch * seq, d)
    tables = _rope_tables(positions, tile)
    for layer in range(depth):
        lambda_init = 0.8 - 0.6 * math.exp(-0.3 * layer)
        qT, k2d, vT, pre, z, ba, gates = _inproj(x2d, norm_mix[layer].reshape(1, d), tables, w_in[layer],
                                                 batch, seq, tile)
        lam = _lambda(lambda_q1[layer], lambda_k1[layer], lambda_q2[layer], lambda_k2[layer], lambda_init)
        subln = (a_subln[layer] * (1.0 - lambda_init)).reshape(-1, 1)
        oa = _attention(lam, qT, k2d, vT, subln, batch, seq, tile)
        ob = _gdn(pre, z, ba, conv_qkv[layer], a_log[layer], dt_bias[layer], b_onorm[layer], seq)
        x2d = _merge(x2d, oa, ob, gates, w_a_out[layer], w_b_out[layer], w_o[layer], tile)
        dff = w_down.shape[1]
        x2d = _ffn(x2d, norm_ffn[layer].reshape(1, d), w_up[layer], ffn_conv[layer],
                   ffn_conv_bias[layer].reshape(1, -1), w_down[layer], norm_final.reshape(1, d), seq, tile,
                   fchunk=256 if dff % 256 == 0 else LANES, apply_final=layer == depth - 1)
    return x2d.reshape(batch, seq, d)
```

```python
import functools
import math

import jax
import jax.numpy as jnp
from jax import lax
from jax.experimental import pallas as pl
from jax.experimental.pallas import tpu as pltpu

NORM_EPS = 1e-6
ROPE_THETA = 10000.0
LANES = 128
MXU_COLS = 256
LOG2E = math.log2(math.e)
VMEM_LIMIT = 56 * 1024 * 1024

A_HEADS = 8
A_HEAD_DIM = 64
A_V_DIM = 128
B_HEADS = 8
B_HEAD_DIM = 128
CONV_WIDTH = 4
FFN_CONV_WIDTH = 3
GDN_CHUNK = 128
HALO = 8
F32 = jnp.float32
BF16 = jnp.bfloat16


def _dot(a, b):
    return jnp.dot(a, b, preferred_element_type=F32)


def _const_spec(shape):
    nd = len(shape)
    return pl.BlockSpec(shape, lambda *_: (0,) * nd, pipeline_mode=pl.Buffered(1))


def _sigmoid(x):
    return 1.0 / (1.0 + jnp.exp(-x))


def _silu(x):
    return x * _sigmoid(x)


def _rope_kernel(pos_ref, posr_ref, freq_ref, sign_ref, freqc_ref, cos_ref, sin_ref, cosT_ref, sinT_ref):
    ang = pos_ref[...].astype(F32) * freq_ref[...]
    cos_ref[...] = jnp.cos(ang)
    sin_ref[...] = jnp.sin(ang) * sign_ref[...]
    angT = freqc_ref[...] * posr_ref[...].astype(F32)
    cosT_ref[...] = jnp.cos(angT)
    sinT_ref[...] = jnp.sin(angT)


def _rope_tables(positions, tile):
    n = positions.size
    half = A_HEAD_DIM // 2
    inv_freq = ROPE_THETA ** (-jnp.arange(0, A_HEAD_DIM, 2, dtype=F32) / A_HEAD_DIM)
    freq = jnp.tile(inv_freq, LANES // half).reshape(1, LANES)
    sign = jnp.tile(jnp.concatenate([-jnp.ones((half,), F32), jnp.ones((half,), F32)]),
                    LANES // A_HEAD_DIM).reshape(1, LANES)
    const = lambda shape: pl.BlockSpec(shape, lambda i: (0, 0))
    return pl.pallas_call(
        _rope_kernel,
        out_shape=(jax.ShapeDtypeStruct((n, LANES), F32), jax.ShapeDtypeStruct((n, LANES), F32),
                   jax.ShapeDtypeStruct((half, n), F32), jax.ShapeDtypeStruct((half, n), F32)),
        grid=(n // tile,),
        in_specs=[pl.BlockSpec((tile, 1), lambda i: (i, 0)), pl.BlockSpec((1, tile), lambda i: (0, i)),
                  const((1, LANES)), const((1, LANES)), const((half, 1))],
        out_specs=(pl.BlockSpec((tile, LANES), lambda i: (i, 0)), pl.BlockSpec((tile, LANES), lambda i: (i, 0)),
                   pl.BlockSpec((half, tile), lambda i: (0, i)), pl.BlockSpec((half, tile), lambda i: (0, i))),
        name="rope_tables",
    )(positions.reshape(n, 1), positions.reshape(1, n), freq, sign, inv_freq.reshape(half, 1))


def _rotary(y, cos, sin_signed):
    half = A_HEAD_DIM // 2
    lane = lax.broadcasted_iota(jnp.int32, (1, LANES), 1)
    first = (lane % A_HEAD_DIM) < half
    outs = []
    for b in range(y.shape[1] // LANES):
        blk = y[:, b * LANES:(b + 1) * LANES]
        partner = jnp.where(first, pltpu.roll(blk, LANES - half, axis=1), pltpu.roll(blk, half, axis=1))
        outs.append(blk * cos + partner * sin_signed)
    return jnp.concatenate(outs, axis=1)


def _rotary_t(y, cos_t, sin_t, scale):
    half = A_HEAD_DIM // 2
    outs = []
    for g in range(y.shape[0] // A_HEAD_DIM):
        x1 = y[g * A_HEAD_DIM:g * A_HEAD_DIM + half]
        x2 = y[g * A_HEAD_DIM + half:(g + 1) * A_HEAD_DIM]
        outs.append((x1 * cos_t - x2 * sin_t) * scale)
        outs.append((x2 * cos_t + x1 * sin_t) * scale)
    return jnp.concatenate(outs, axis=0)


def _dot_nt(a, b):
    return lax.dot_general(a, b, (((1,), (1,)), ((), ())), preferred_element_type=F32)


def _inproj_kernel(x_ref, nw_ref, cos_ref, sin_ref, cosT_ref, sinT_ref, wqT_ref, wk_ref, wvT_ref, wpre_ref, wz_ref,
                   wba_ref, wgate_ref, cw_ref, qT_ref, k_ref, vT_ref, qkv_ref, z_ref, ba_ref, gate_ref, carry_ref, *,
                   tiles_per_seq, chunk):
    tile = x_ref.shape[0]

    @pl.when(pl.program_id(0) % tiles_per_seq == 0)
    def _():
        carry_ref[...] = jnp.zeros(carry_ref.shape, F32)

    x = x_ref[...]
    h = (x * lax.rsqrt(jnp.mean(x * x, axis=-1, keepdims=True) + NORM_EPS) * nw_ref[...]).astype(BF16)
    ba_ref[...] = _dot(h, wba_ref[...])

    cos_t, sin_t = cosT_ref[...], sinT_ref[...]
    cos, sin = cos_ref[...], sin_ref[...]

    def q_job(sl):
        def fin(u):
            qT_ref[0, 0, sl, :] = _rotary_t(u, cos_t, sin_t, A_HEAD_DIM ** -0.5 * LOG2E).astype(BF16)
        return (lambda: _dot_nt(wqT_ref[sl, :], h)), fin

    def v_job(sl):
        def fin(u):
            vT_ref[0, 0, sl, :] = u.astype(BF16)
        return (lambda: _dot_nt(wvT_ref[sl, :], h)), fin

    def k_job(sl):
        def fin(u):
            k_ref[:, sl] = _rotary(u, cos, sin).astype(BF16)
        return (lambda: _dot(h, wk_ref[:, sl])), fin

    def plain_job(w_ref, o_ref, sl):
        def fin(u):
            o_ref[:, sl] = u.astype(o_ref.dtype)
        return (lambda: _dot(h, w_ref[:, sl])), fin

    def conv_job(sl):
        def fin(u):
            ext = jnp.concatenate([carry_ref[:, sl], u], axis=0)
            cw = cw_ref[:, sl]
            y = cw[CONV_WIDTH - 1:CONV_WIDTH] * u
            for s in range(1, CONV_WIDTH):
                y = y + cw[CONV_WIDTH - 1 - s:CONV_WIDTH - s] * pltpu.roll(ext, s, axis=0)[HALO:]
            carry_ref[:, sl] = u[tile - HALO:, :]
            qkv_ref[:, sl] = _silu(y).astype(qkv_ref.dtype)
        return (lambda: _dot(h, wpre_ref[:, sl])), fin

    chunks = lambda width: [slice(c * chunk, (c + 1) * chunk) for c in range(width // chunk)]
    light = ([q_job(sl) for sl in chunks(wqT_ref.shape[0])] + [k_job(sl) for sl in chunks(wk_ref.shape[1])]
             + [v_job(sl) for sl in chunks(wvT_ref.shape[0])]
             + [plain_job(wz_ref, z_ref, sl) for sl in chunks(wz_ref.shape[1])]
             + [plain_job(wgate_ref, gate_ref, sl) for sl in chunks(wgate_ref.shape[1])])
    heavy = [conv_job(sl) for sl in chunks(wpre_ref.shape[1])]
    per = max(1, len(light) // len(heavy))
    jobs = []
    for n, hv in enumerate(heavy):
        jobs += light[n * per:(n + 1) * per] + [hv]
    jobs += light[len(heavy) * per:]
    u_next = jobs[0][0]()
    for n, (_, fin) in enumerate(jobs):
        u = u_next
        if n + 1 < len(jobs):
            u_next = jobs[n + 1][0]()
        fin(u)


def _inproj(x2d, nw, tables, w_in, conv_w, batch, seq, tile):
    n, d = x2d.shape
    nt = seq // tile
    aw = A_HEADS * 2 * A_HEAD_DIM
    bw = B_HEADS * B_HEAD_DIM
    half = A_HEAD_DIM // 2
    cos, sin, cos_t, sin_t = tables
    wb = w_in.astype(BF16)
    o = 0
    wq_t = wb[:, o:o + aw].T; o += aw
    wk = wb[:, o:o + aw]; o += aw
    wv_t = wb[:, o:o + aw].T; o += aw
    wpre = wb[:, o:o + 3 * bw]; o += 3 * bw
    wz = wb[:, o:o + bw]; o += bw
    wba = jnp.pad(wb[:, o:o + 2 * B_HEADS], ((0, 0), (0, LANES - 2 * B_HEADS))); o += 2 * B_HEADS
    wgate = wb[:, o:o + 2 * d]
    row = lambda w: pl.BlockSpec((tile, w), lambda i: (i, 0))
    col = pl.BlockSpec((half, tile), lambda i: (0, i))
    tspec = pl.BlockSpec((1, 1, aw, tile), lambda i: (i // nt, i % nt, 0, 0))
    chunk = MXU_COLS if bw % MXU_COLS == 0 and d % MXU_COLS == 0 else LANES
    return pl.pallas_call(
        functools.partial(_inproj_kernel, tiles_per_seq=nt, chunk=chunk),
        out_shape=(jax.ShapeDtypeStruct((batch, nt, aw, tile), BF16),
                   jax.ShapeDtypeStruct((n, aw), BF16),
                   jax.ShapeDtypeStruct((batch, nt, aw, tile), BF16),
                   jax.ShapeDtypeStruct((n, 3 * bw), BF16),
                   jax.ShapeDtypeStruct((n, bw), BF16),
                   jax.ShapeDtypeStruct((n, LANES), F32),
                   jax.ShapeDtypeStruct((n, 2 * d), BF16)),
        grid=(n // tile,),
        in_specs=[row(d), _const_spec((1, d)), row(LANES), row(LANES), col, col,
                  _const_spec(wq_t.shape), _const_spec(wk.shape), _const_spec(wv_t.shape),
                  _const_spec(wpre.shape), _const_spec(wz.shape), _const_spec(wba.shape),
                  _const_spec(wgate.shape), _const_spec(conv_w.shape)],
        out_specs=(tspec, row(aw), tspec, row(3 * bw), row(bw), row(LANES), row(2 * d)),
        scratch_shapes=[pltpu.VMEM((HALO, 3 * bw), F32)],
        compiler_params=pltpu.CompilerParams(dimension_semantics=("arbitrary",), vmem_limit_bytes=VMEM_LIMIT),
        name="inproj",
    )(x2d, nw, cos, sin, cos_t, sin_t, wq_t, wk, wv_t, wpre, wz, wba, wgate, conv_w)


def _attn_kernel(lam_ref, qT_ref, k_ref, vT_ref, subln_ref, o_ref, qm_ref, m_ref, l_ref, acc_ref, s_ref, mt_ref, *,
                 tile, cb, hp):
    i = pl.program_id(2)
    dh = A_HEAD_DIM
    dv = A_V_DIM
    zero = jnp.zeros((dh, tile), BF16)
    for hh in range(hp):
        qT = qT_ref[0, 0, hh * 2 * dh:(hh + 1) * 2 * dh, :]
        qm_ref[:, hh * 2 * tile:hh * 2 * tile + tile] = jnp.concatenate([qT[:dh], zero], axis=0)
        qm_ref[:, hh * 2 * tile + tile:(hh + 1) * 2 * tile] = jnp.concatenate([zero, qT[dh:]], axis=0)
    m_ref[...] = jnp.full(m_ref.shape, -jnp.inf, F32)
    l_ref[...] = jnp.zeros(l_ref.shape, F32)
    acc_ref[...] = jnp.zeros(acc_ref.shape, F32)

    ncb = 2 * tile // cb

    def step(cur, nxt, mask_next):
        for c in range(hp * ncb):
            hh = c // ncb
            sl = slice(c * cb, (c + 1) * cb)
            if nxt is not None:
                k_next = k_ref[0, pl.ds(pl.multiple_of(nxt * tile, tile), tile), hh * 2 * dh:(hh + 1) * 2 * dh]
                s_new = _dot(k_next, qm_ref[:, sl])
            if cur is not None:
                m_prev = m_ref[:, sl]
                m_new = jnp.maximum(m_prev, mt_ref[:, sl])
                alpha = jnp.exp2(m_prev - m_new)
                p = jnp.exp2(s_ref[:, sl] - m_new)
                l_ref[:, sl] = alpha * l_ref[:, sl] + jnp.sum(p, axis=0, keepdims=True)
                v_cur = vT_ref[0, cur, hh * dv:(hh + 1) * dv, :]
                acc_ref[:, sl] = alpha * acc_ref[:, sl] + _dot(v_cur, p.astype(BF16))
                m_ref[:, sl] = m_new
            if nxt is not None:
                if mask_next:
                    kpos = lax.broadcasted_iota(jnp.int32, (tile, cb), 0)
                    qpos = lax.broadcasted_iota(jnp.int32, (tile, cb), 1) + (c * cb) % tile
                    s_new = jnp.where(kpos <= qpos, s_new, -jnp.inf)
                s_ref[:, sl] = s_new
                mt_ref[:, sl] = jnp.max(s_new, axis=0, keepdims=True)

    @pl.when(i == 0)
    def _():
        step(None, 0, True)

    @pl.when(i > 0)
    def _():
        step(None, 0, False)

    def body(j, carry):
        step(j, j + 1, False)
        return carry

    lax.fori_loop(0, i - 1, body, 0)

    @pl.when(i > 0)
    def _():
        step(i - 1, i, True)

    step(i, None, False)

    lam = lam_ref[0, 0]
    inv_l = 1.0 / l_ref[...]
    for hh in range(hp):
        c1 = slice(hh * 2 * tile, hh * 2 * tile + tile)
        c2 = slice(hh * 2 * tile + tile, (hh + 1) * 2 * tile)
        o = acc_ref[:, c1] * inv_l[:, c1] - lam * (acc_ref[:, c2] * inv_l[:, c2])
        o = o * lax.rsqrt(jnp.mean(o * o, axis=0, keepdims=True) + NORM_EPS) * subln_ref[...]
        o_ref[:, hh * dv:(hh + 1) * dv] = o.T.astype(o_ref.dtype)


def _attention(lam, qT, k2d, vT, subln_col, batch, seq, tile):
    nt = seq // tile
    dv = A_V_DIM
    n = batch * seq
    hp = 2
    assert A_HEADS % hp == 0
    kern = functools.partial(_attn_kernel, tile=tile, cb=min(tile, MXU_COLS), hp=hp)
    lanes = hp * 2 * tile
    return pl.pallas_call(
        kern,
        out_shape=jax.ShapeDtypeStruct((n, A_HEADS * dv), BF16),
        grid=(batch, A_HEADS // hp, nt),
        in_specs=[pl.BlockSpec(memory_space=pltpu.SMEM),
                  pl.BlockSpec((1, 1, hp * 2 * A_HEAD_DIM, tile), lambda b, h, i: (b, i, h, 0)),
                  pl.BlockSpec((1, seq, hp * 2 * A_HEAD_DIM), lambda b, h, i: (b, 0, h)),
                  pl.BlockSpec((1, nt, hp * dv, tile), lambda b, h, i: (b, 0, h, 0)),
                  pl.BlockSpec((dv, 1), lambda b, h, i: (0, 0))],
        out_specs=pl.BlockSpec((tile, hp * dv), lambda b, h, i: (b * nt + i, h)),
        scratch_shapes=[pltpu.VMEM((2 * A_HEAD_DIM, lanes), BF16),
                        pltpu.VMEM((1, lanes), F32),
                        pltpu.VMEM((1, lanes), F32),
                        pltpu.VMEM((dv, lanes), F32),
                        pltpu.VMEM((tile, lanes), F32),
                        pltpu.VMEM((1, lanes), F32)],
        compiler_params=pltpu.CompilerParams(dimension_semantics=("arbitrary",) * 3, vmem_limit_bytes=VMEM_LIMIT),
        name="diff_attention",
    )(lam, qT, k2d.reshape(batch, seq, -1), vT, subln_col)


def _merge_kernel(x_ref, oa_ref, ob_ref, gate_ref, wa_ref, wb_ref, wo_ref, o_ref):
    d = x_ref.shape[1]
    ya = _dot(oa_ref[...], wa_ref[...])
    yb = _dot(ob_ref[...], wb_ref[...])
    g = gate_ref[...].astype(F32)
    merged = _sigmoid(g[:, :d]) * ya + _sigmoid(g[:, d:]) * yb
    o_ref[...] = x_ref[...] + _dot(merged.astype(BF16), wo_ref[...])


def _merge(x2d, oa, ob, gates, wa, wb, wo, tile):
    n, d = x2d.shape
    row = lambda w: pl.BlockSpec((tile, w), lambda i: (i, 0))
    return pl.pallas_call(
        _merge_kernel,
        out_shape=jax.ShapeDtypeStruct((n, d), F32),
        grid=(n // tile,),
        in_specs=[row(d), row(oa.shape[1]), row(ob.shape[1]), row(2 * d),
                  _const_spec(wa.shape), _const_spec(wb.shape), _const_spec(wo.shape)],
        out_specs=row(d),
        compiler_params=pltpu.CompilerParams(dimension_semantics=("arbitrary",), vmem_limit_bytes=VMEM_LIMIT),
        name="merge_outproj",
    )(x2d, oa, ob, gates, wa.astype(BF16), wb.astype(BF16), wo.astype(BF16))


def _ffn_kernel(x_ref, nw_ref, wup_ref, cw_ref, cb_ref, wdown_ref, nf_ref, o_ref, ucar_ref, act_ref, *, tiles_per_seq,
                fchunk, apply_final):
    i = pl.program_id(0)
    tile, d = x_ref.shape
    dff = wdown_ref.shape[0]

    @pl.when(i % tiles_per_seq == 0)
    def _():
        ucar_ref[...] = jnp.zeros(ucar_ref.shape, F32)

    x = x_ref[...]
    h = (x * lax.rsqrt(jnp.mean(x * x, axis=-1, keepdims=True) + NORM_EPS) * nw_ref[...]).astype(BF16)

    def up(c):
        return [_dot(h, wup_ref[:, base + c * fchunk:base + (c + 1) * fchunk]) for base in (0, dff)]

    u_next = up(0)
    for c in range(dff // fchunk):
        u_cur = u_next
        if c + 1 < dff // fchunk:
            u_next = up(c + 1)
        halves = []
        for u, base in zip(u_cur, (0, dff)):
            sl = slice(base + c * fchunk, base + (c + 1) * fchunk)
            uext = jnp.concatenate([ucar_ref[:, sl], u], axis=0)
            cw = cw_ref[:, sl]
            y = cb_ref[:, sl] + cw[FFN_CONV_WIDTH - 1:FFN_CONV_WIDTH] * u
            for s in range(1, FFN_CONV_WIDTH):
                y = y + cw[FFN_CONV_WIDTH - 1 - s:FFN_CONV_WIDTH - s] * pltpu.roll(uext, s, axis=0)[HALO:]
            ucar_ref[:, sl] = u[tile - HALO:, :]
            halves.append(y)
        act_ref[:, c * fchunk:(c + 1) * fchunk] = (_silu(halves[0]) * halves[1]).astype(BF16)
    x2 = x + _dot(act_ref[...], wdown_ref[...])
    if apply_final:
        x2 = x2 * lax.rsqrt(jnp.mean(x2 * x2, axis=-1, keepdims=True) + NORM_EPS) * nf_ref[...]
    o_ref[...] = x2


def _ffn(x2d, nw, wup, cw, cb, wdown, nf, seq, tile, fchunk, apply_final):
    n, d = x2d.shape
    dff = wdown.shape[0]
    row = pl.BlockSpec((tile, d), lambda i: (i, 0))
    kern = functools.partial(_ffn_kernel, tiles_per_seq=seq // tile, fchunk=fchunk, apply_final=apply_final)
    return pl.pallas_call(
        kern,
        out_shape=jax.ShapeDtypeStruct((n, d), F32),
        grid=(n // tile,),
        in_specs=[row, _const_spec((1, d)), _const_spec(wup.shape), _const_spec(cw.shape),
                  _const_spec((1, 2 * dff)), _const_spec(wdown.shape), _const_spec((1, d))],
        out_specs=row,
        scratch_shapes=[pltpu.VMEM((HALO, 2 * dff), F32), pltpu.VMEM((tile, dff), BF16)],
        compiler_params=pltpu.CompilerParams(dimension_semantics=("arbitrary",), vmem_limit_bytes=VMEM_LIMIT),
        name="convffn",
    )(x2d, nw, wup.astype(BF16), cw, cb, wdown.astype(BF16), nf)


def _gdn_kernel(qkv_ref, z_ref, ba_ref, baT_ref, alog_ref, dtb_ref, alogc_ref, dtbc_ref, onorm_ref,
                o_ref, state_ref, *, nb):
    c = GDN_CHUNK
    dk = B_HEAD_DIM
    bw = B_HEADS * dk

    @pl.when(pl.program_id(1) == 0)
    def _():
        state_ref[...] = jnp.zeros(state_ref.shape, F32)

    r = lax.broadcasted_iota(jnp.int32, (c, c), 0)
    q_ = lax.broadcasted_iota(jnp.int32, (c, c), 1)
    lower = r >= q_
    strict = r > q_
    tri_lo = lower.astype(F32)
    tri_up = (r <= q_).astype(F32)
    softplus = lambda t: jnp.maximum(t, 0.0) + jnp.log1p(jnp.exp(-jnp.abs(t)))
    nt = (((1,), (1,)), ((), ()))
    tn = (((0,), (0,)), ((), ()))

    qs, ks, vs, bhs, gchs, glasts, decays = [], [], [], [], [], [], []
    for s_i in range(nb):
        qkv = qkv_ref[s_i].astype(F32)
        ba = ba_ref[s_i]
        baT = baT_ref[s_i, 0]
        beta = _sigmoid(ba)
        g = -jnp.exp(alog_ref[...]) * softplus(ba + dtb_ref[...])
        gT = -jnp.exp(alogc_ref[...]) * softplus(baT + dtbc_ref[...])
        gc = jnp.dot(tri_lo, g, precision=lax.Precision.HIGHEST, preferred_element_type=F32)
        gcT = jnp.dot(gT, tri_up, precision=lax.Precision.HIGHEST, preferred_element_type=F32)
        for h in range(B_HEADS):
            qh = qkv[:, h * dk:(h + 1) * dk]
            kh = qkv[:, bw + h * dk:bw + (h + 1) * dk]
            qs.append(qh * (lax.rsqrt(jnp.sum(qh * qh, axis=-1, keepdims=True) + NORM_EPS) * (dk ** -0.5)))
            ks.append(kh * lax.rsqrt(jnp.sum(kh * kh, axis=-1, keepdims=True) + NORM_EPS))
            vs.append(qkv[:, 2 * bw + h * dk:2 * bw + (h + 1) * dk])
            gl = B_HEADS + h
            bhs.append(beta[:, h:h + 1])
            gchs.append(gc[:, gl:gl + 1])
            glasts.append(gc[c - 1:c, gl:gl + 1])
            diff = gc[:, gl:gl + 1] - gcT[gl:gl + 1, :]
            decays.append(jnp.where(lower, jnp.exp(jnp.where(lower, diff, 0.0)), 0.0))

    chains = range(nb * B_HEADS)
    kbs = [ks[i] * bhs[i] for i in chains]
    kfs = [ks[i].astype(BF16) for i in chains]
    kq = [lax.dot_general(jnp.concatenate([kbs[i], qs[i]], axis=0).astype(BF16), kfs[i], nt,
                          preferred_element_type=F32) for i in chains]
    a_mats = [jnp.where(strict, kq[i][:c] * decays[i], 0.0) for i in chains]
    qks = [jnp.where(lower, kq[i][c:] * decays[i], 0.0).astype(BF16) for i in chains]
    xs = [jnp.concatenate([vs[i] * bhs[i], kbs[i] * jnp.exp(gchs[i])], axis=1) for i in chains]
    pws = [a.astype(BF16) for a in a_mats]
    xs = [xs[i] - _dot(pws[i], xs[i].astype(BF16)) for i in chains]
    for _ in range(int(math.log2(c)) - 1):
        pws = [_dot(p, p).astype(BF16) for p in pws]
        xs = [xs[i] + _dot(pws[i], xs[i].astype(BF16)) for i in chains]
    states = [state_ref[i] for i in chains]
    wq = [jnp.concatenate([xs[i][:, dk:], qs[i] * jnp.exp(gchs[i])], axis=0).astype(BF16) for i in chains]
    ws = [_dot(wq[i], states[i].astype(BF16)) for i in chains]
    v_new = [(xs[i][:, :dk] - ws[i][:c]).astype(BF16) for i in chains]
    kdec = [(ks[i] * jnp.exp(glasts[i] - gchs[i])).astype(BF16) for i in chains]
    for i in chains:
        state_ref[i] = states[i] * jnp.exp(glasts[i]) + lax.dot_general(kdec[i], v_new[i], tn,
                                                                        preferred_element_type=F32)
    outs = [ws[i][c:] + _dot(qks[i], v_new[i]) for i in chains]
    onorm = onorm_ref[...]
    for s_i in range(nb):
        z = z_ref[s_i].astype(F32)
        for h in range(B_HEADS):
            o = outs[s_i * B_HEADS + h]
            o = o * lax.rsqrt(jnp.mean(o * o, axis=-1, keepdims=True) + NORM_EPS) * onorm
            o_ref[s_i, :, h * dk:(h + 1) * dk] = (o * _silu(z[:, h * dk:(h + 1) * dk])).astype(o_ref.dtype)


def _gdn(qkv, z, ba, a_log, dt_bias, onorm, batch, seq):
    c = GDN_CHUNK
    bw = B_HEADS * B_HEAD_DIM
    nb = 2 if batch % 2 == 0 else 1
    nchunk = seq // c
    baT = ba.reshape(batch, nchunk, c, LANES).transpose(0, 1, 3, 2)
    pad = lambda v: jnp.pad(v.astype(F32), (B_HEADS, LANES - 2 * B_HEADS))
    alog, dtb = pad(a_log), pad(dt_bias)
    row = lambda w: pl.BlockSpec((nb, c, w), lambda b, t: (b, t, 0))
    out = pl.pallas_call(
        functools.partial(_gdn_kernel, nb=nb),
        out_shape=jax.ShapeDtypeStruct((batch, seq, bw), BF16),
        grid=(batch // nb, nchunk),
        in_specs=[row(3 * bw), row(bw), row(LANES), pl.BlockSpec((nb, 1, LANES, c), lambda b, t: (b, t, 0, 0)),
                  _const_spec((1, LANES)), _const_spec((1, LANES)),
                  _const_spec((LANES, 1)), _const_spec((LANES, 1)), _const_spec((1, B_HEAD_DIM))],
        out_specs=row(bw),
        scratch_shapes=[pltpu.VMEM((nb * B_HEADS, B_HEAD_DIM, B_HEAD_DIM), F32)],
        compiler_params=pltpu.CompilerParams(dimension_semantics=("arbitrary", "arbitrary"),
                                             vmem_limit_bytes=VMEM_LIMIT),
        name="gated_deltanet",
    )(qkv.reshape(batch, seq, -1), z.reshape(batch, seq, -1), ba.reshape(batch, seq, -1), baT,
      alog.reshape(1, -1), dtb.reshape(1, -1), alog.reshape(-1, 1), dtb.reshape(-1, 1), onorm.reshape(1, -1))
    return out.reshape(batch * seq, bw)


def _lam_kernel(q1_ref, k1_ref, q2_ref, k2_ref, o_ref, *, lambda_init):
    s1 = jnp.sum(q1_ref[...] * k1_ref[...], axis=-1, keepdims=True)
    s2 = jnp.sum(q2_ref[...] * k2_ref[...], axis=-1, keepdims=True)
    o_ref[...] = jnp.exp(s1) - jnp.exp(s2) + lambda_init


def _lambda(q1, k1, q2, k2, lambda_init):
    spec = pl.BlockSpec((1, q1.shape[-1]), lambda: (0, 0))
    return pl.pallas_call(
        functools.partial(_lam_kernel, lambda_init=lambda_init),
        out_shape=jax.ShapeDtypeStruct((1, 1), F32),
        in_specs=[spec] * 4,
        out_specs=pl.BlockSpec((1, 1), lambda: (0, 0)),
        name="lambda",
    )(q1.reshape(1, -1), k1.reshape(1, -1), q2.reshape(1, -1), k2.reshape(1, -1))


def _pick_tile(seq, pref):
    t = min(seq, pref)
    assert seq % t == 0
    return t


def kernel(x, positions, norm_mix, w_in, lambda_q1, lambda_k1, lambda_q2, lambda_k2, a_subln, w_a_out, conv_qkv,
           a_log, dt_bias, b_onorm, w_b_out, w_o, norm_ffn, w_up, ffn_conv, ffn_conv_bias, w_down, norm_final):
    batch, seq, d = x.shape
    depth = w_in.shape[0]
    tile = _pick_tile(seq, 512)
    x2d = x.reshape(batch * seq, d)
    tables = _rope_tables(positions, tile)
    for layer in range(depth):
        lambda_init = 0.8 - 0.6 * math.exp(-0.3 * layer)
        qT, k2d, vT, qkv, z, ba, gates = _inproj(x2d, norm_mix[layer].reshape(1, d), tables, w_in[layer],
                                                 conv_qkv[layer], batch, seq, tile)
        lam = _lambda(lambda_q1[layer], lambda_k1[layer], lambda_q2[layer], lambda_k2[layer], lambda_init)
        subln = (a_subln[layer] * (1.0 - lambda_init)).reshape(-1, 1)
        oa = _attention(lam, qT, k2d, vT, subln, batch, seq, tile)
        ob = _gdn(qkv, z, ba, a_log[layer], dt_bias[layer], b_onorm[layer], batch, seq)
        x2d = _merge(x2d, oa, ob, gates, w_a_out[layer], w_b_out[layer], w_o[layer], tile)
        dff = w_down.shape[1]
        x2d = _ffn(x2d, norm_ffn[layer].reshape(1, d), w_up[layer], ffn_conv[layer],
                   ffn_conv_bias[layer].reshape(1, -1), w_down[layer], norm_final.reshape(1, d), seq, tile,
                   fchunk=256 if dff % 256 == 0 else LANES, apply_final=layer == depth - 1)
    return x2d.reshape(batch, seq, d)
```

```python
import functools
import math

import jax
import jax.numpy as jnp
from jax import lax
from jax.experimental import pallas as pl
from jax.experimental.pallas import tpu as pltpu

NORM_EPS = 1e-6
ROPE_THETA = 10000.0
LANES = 128
MXU_COLS = 256
LOG2E = math.log2(math.e)
VMEM_LIMIT = 56 * 1024 * 1024

A_HEADS = 8
A_HEAD_DIM = 64
A_V_DIM = 128
B_HEADS = 8
B_HEAD_DIM = 128
CONV_WIDTH = 4
FFN_CONV_WIDTH = 3
GDN_CHUNK = 128
HALO = 8
F32 = jnp.float32
BF16 = jnp.bfloat16


def _dot(a, b):
    return jnp.dot(a, b, preferred_element_type=F32)


def _const_spec(shape):
    nd = len(shape)
    return pl.BlockSpec(shape, lambda *_: (0,) * nd, pipeline_mode=pl.Buffered(1))


def _sigmoid(x):
    return 1.0 / (1.0 + jnp.exp(-x))


def _silu(x):
    return x * _sigmoid(x)


def _rope_kernel(pos_ref, posr_ref, freq_ref, sign_ref, freqc_ref, cos_ref, sin_ref, cosT_ref, sinT_ref):
    ang = pos_ref[...].astype(F32) * freq_ref[...]
    cos_ref[...] = jnp.cos(ang)
    sin_ref[...] = jnp.sin(ang) * sign_ref[...]
    angT = freqc_ref[...] * posr_ref[...].astype(F32)
    cosT_ref[...] = jnp.cos(angT)
    sinT_ref[...] = jnp.sin(angT)


def _rope_tables(positions, tile):
    n = positions.size
    half = A_HEAD_DIM // 2
    inv_freq = ROPE_THETA ** (-jnp.arange(0, A_HEAD_DIM, 2, dtype=F32) / A_HEAD_DIM)
    freq = jnp.tile(inv_freq, LANES // half).reshape(1, LANES)
    sign = jnp.tile(jnp.concatenate([-jnp.ones((half,), F32), jnp.ones((half,), F32)]),
                    LANES // A_HEAD_DIM).reshape(1, LANES)
    const = lambda shape: pl.BlockSpec(shape, lambda i: (0, 0))
    return pl.pallas_call(
        _rope_kernel,
        out_shape=(jax.ShapeDtypeStruct((n, LANES), F32), jax.ShapeDtypeStruct((n, LANES), F32),
                   jax.ShapeDtypeStruct((half, n), F32), jax.ShapeDtypeStruct((half, n), F32)),
        grid=(n // tile,),
        in_specs=[pl.BlockSpec((tile, 1), lambda i: (i, 0)), pl.BlockSpec((1, tile), lambda i: (0, i)),
                  const((1, LANES)), const((1, LANES)), const((half, 1))],
        out_specs=(pl.BlockSpec((tile, LANES), lambda i: (i, 0)), pl.BlockSpec((tile, LANES), lambda i: (i, 0)),
                   pl.BlockSpec((half, tile), lambda i: (0, i)), pl.BlockSpec((half, tile), lambda i: (0, i))),
        name="rope_tables",
    )(positions.reshape(n, 1), positions.reshape(1, n), freq, sign, inv_freq.reshape(half, 1))


def _rotary(y, cos, sin_signed):
    half = A_HEAD_DIM // 2
    lane = lax.broadcasted_iota(jnp.int32, (1, LANES), 1)
    first = (lane % A_HEAD_DIM) < half
    outs = []
    for b in range(y.shape[1] // LANES):
        blk = y[:, b * LANES:(b + 1) * LANES]
        partner = jnp.where(first, pltpu.roll(blk, LANES - half, axis=1), pltpu.roll(blk, half, axis=1))
        outs.append(blk * cos + partner * sin_signed)
    return jnp.concatenate(outs, axis=1)


def _rotary_t(y, cos_t, sin_t, scale):
    half = A_HEAD_DIM // 2
    outs = []
    for g in range(y.shape[0] // A_HEAD_DIM):
        x1 = y[g * A_HEAD_DIM:g * A_HEAD_DIM + half]
        x2 = y[g * A_HEAD_DIM + half:(g + 1) * A_HEAD_DIM]
        outs.append((x1 * cos_t - x2 * sin_t) * scale)
        outs.append((x2 * cos_t + x1 * sin_t) * scale)
    return jnp.concatenate(outs, axis=0)


def _dot_nt(a, b):
    return lax.dot_general(a, b, (((1,), (1,)), ((), ())), preferred_element_type=F32)


def _inproj_kernel(x_ref, nw_ref, cos_ref, sin_ref, cosT_ref, sinT_ref, wqT_ref, wk_ref, wvT_ref, wpre_ref, wz_ref,
                   wba_ref, wgate_ref, cw_ref, qT_ref, k_ref, vT_ref, qkv_ref, z_ref, ba_ref, gate_ref, carry_ref, *,
                   tiles_per_seq, chunk):
    tile = x_ref.shape[0]

    @pl.when(pl.program_id(0) % tiles_per_seq == 0)
    def _():
        carry_ref[...] = jnp.zeros(carry_ref.shape, F32)

    x = x_ref[...]
    h = (x * lax.rsqrt(jnp.mean(x * x, axis=-1, keepdims=True) + NORM_EPS) * nw_ref[...]).astype(BF16)
    ba_ref[...] = _dot(h, wba_ref[...])

    cos_t, sin_t = cosT_ref[...], sinT_ref[...]
    cos, sin = cos_ref[...], sin_ref[...]

    def q_job(sl):
        def fin(u):
            qT_ref[0, 0, sl, :] = _rotary_t(u, cos_t, sin_t, A_HEAD_DIM ** -0.5 * LOG2E).astype(BF16)
        return (lambda: _dot_nt(wqT_ref[sl, :], h)), fin

    def v_job(sl):
        def fin(u):
            vT_ref[0, 0, sl, :] = u.astype(BF16)
        return (lambda: _dot_nt(wvT_ref[sl, :], h)), fin

    def k_job(sl):
        def fin(u):
            k_ref[:, sl] = _rotary(u, cos, sin).astype(BF16)
        return (lambda: _dot(h, wk_ref[:, sl])), fin

    def plain_job(w_ref, o_ref, sl):
        def fin(u):
            o_ref[:, sl] = u.astype(o_ref.dtype)
        return (lambda: _dot(h, w_ref[:, sl])), fin

    def conv_job(sl):
        def fin(u):
            ext = jnp.concatenate([carry_ref[:, sl], u], axis=0)
            cw = cw_ref[:, sl]
            y = cw[CONV_WIDTH - 1:CONV_WIDTH] * u
            for s in range(1, CONV_WIDTH):
                y = y + cw[CONV_WIDTH - 1 - s:CONV_WIDTH - s] * pltpu.roll(ext, s, axis=0)[HALO:]
            carry_ref[:, sl] = u[tile - HALO:, :]
            qkv_ref[:, sl] = _silu(y).astype(qkv_ref.dtype)
        return (lambda: _dot(h, wpre_ref[:, sl])), fin

    chunks = lambda width: [slice(c * chunk, (c + 1) * chunk) for c in range(width // chunk)]
    light = ([q_job(sl) for sl in chunks(wqT_ref.shape[0])] + [k_job(sl) for sl in chunks(wk_ref.shape[1])]
             + [v_job(sl) for sl in chunks(wvT_ref.shape[0])]
             + [plain_job(wz_ref, z_ref, sl) for sl in chunks(wz_ref.shape[1])]
             + [plain_job(wgate_ref, gate_ref, sl) for sl in chunks(wgate_ref.shape[1])])
    heavy = [conv_job(sl) for sl in chunks(wpre_ref.shape[1])]
    per = max(1, len(light) // len(heavy))
    jobs = []
    for n, hv in enumerate(heavy):
        jobs += light[n * per:(n + 1) * per] + [hv]
    jobs += light[len(heavy) * per:]
    u_next = jobs[0][0]()
    for n, (_, fin) in enumerate(jobs):
        u = u_next
        if n + 1 < len(jobs):
            u_next = jobs[n + 1][0]()
        fin(u)


def _inproj(x2d, nw, tables, w_in, conv_w, batch, seq, tile):
    n, d = x2d.shape
    nt = seq // tile
    aw = A_HEADS * 2 * A_HEAD_DIM
    bw = B_HEADS * B_HEAD_DIM
    half = A_HEAD_DIM // 2
    cos, sin, cos_t, sin_t = tables
    wb = w_in.astype(BF16)
    o = 0
    wq_t = wb[:, o:o + aw].T; o += aw
    wk = wb[:, o:o + aw]; o += aw
    wv_t = wb[:, o:o + aw].T; o += aw
    wpre = wb[:, o:o + 3 * bw]; o += 3 * bw
    wz = wb[:, o:o + bw]; o += bw
    wba = jnp.pad(wb[:, o:o + 2 * B_HEADS], ((0, 0), (0, LANES - 2 * B_HEADS))); o += 2 * B_HEADS
    wgate = wb[:, o:o + 2 * d]
    row = lambda w: pl.BlockSpec((tile, w), lambda i: (i, 0))
    col = pl.BlockSpec((half, tile), lambda i: (0, i))
    tspec = pl.BlockSpec((1, 1, aw, tile), lambda i: (i // nt, i % nt, 0, 0))
    chunk = MXU_COLS if bw % MXU_COLS == 0 and d % MXU_COLS == 0 else LANES
    return pl.pallas_call(
        functools.partial(_inproj_kernel, tiles_per_seq=nt, chunk=chunk),
        out_shape=(jax.ShapeDtypeStruct((batch, nt, aw, tile), BF16),
                   jax.ShapeDtypeStruct((n, aw), BF16),
                   jax.ShapeDtypeStruct((batch, nt, aw, tile), BF16),
                   jax.ShapeDtypeStruct((n, 3 * bw), BF16),
                   jax.ShapeDtypeStruct((n, bw), BF16),
                   jax.ShapeDtypeStruct((n, LANES), F32),
                   jax.ShapeDtypeStruct((n, 2 * d), BF16)),
        grid=(n // tile,),
        in_specs=[row(d), _const_spec((1, d)), row(LANES), row(LANES), col, col,
                  _const_spec(wq_t.shape), _const_spec(wk.shape), _const_spec(wv_t.shape),
                  _const_spec(wpre.shape), _const_spec(wz.shape), _const_spec(wba.shape),
                  _const_spec(wgate.shape), _const_spec(conv_w.shape)],
        out_specs=(tspec, row(aw), tspec, row(3 * bw), row(bw), row(LANES), row(2 * d)),
        scratch_shapes=[pltpu.VMEM((HALO, 3 * bw), F32)],
        compiler_params=pltpu.CompilerParams(dimension_semantics=("arbitrary",), vmem_limit_bytes=VMEM_LIMIT),
        name="inproj",
    )(x2d, nw, cos, sin, cos_t, sin_t, wq_t, wk, wv_t, wpre, wz, wba, wgate, conv_w)


def _attn_kernel(lam_ref, qT_ref, k_ref, vT_ref, subln_ref, o_ref, qm_ref, m_ref, l_ref, acc_ref, s_ref, mt_ref, *,
                 tile, cb, hp, nt):
    dh = A_HEAD_DIM
    dv = A_V_DIM
    ncb = 2 * tile // cb
    zero = jnp.zeros((dh, tile), BF16)

    def load_q(i):
        for hh in range(hp):
            qT = qT_ref[0, i, hh * 2 * dh:(hh + 1) * 2 * dh, :]
            qm_ref[:, hh * 2 * tile:hh * 2 * tile + tile] = jnp.concatenate([qT[:dh], zero], axis=0)
            qm_ref[:, hh * 2 * tile + tile:(hh + 1) * 2 * tile] = jnp.concatenate([zero, qT[dh:]], axis=0)

    def reset():
        m_ref[...] = jnp.full(m_ref.shape, -jnp.inf, F32)
        l_ref[...] = jnp.zeros(l_ref.shape, F32)
        acc_ref[...] = jnp.zeros(acc_ref.shape, F32)

    def step(cur, nxt, nxt_diag, cur_diag):
        for c in range(hp * ncb):
            hh = c // ncb
            sl = slice(c * cb, (c + 1) * cb)
            qo = (c * cb) % tile
            rows_n = min(tile, qo + cb) if nxt_diag else tile
            rows_c = min(tile, qo + cb) if cur_diag else tile
            if nxt is not None:
                k_next = k_ref[0, pl.ds(pl.multiple_of(nxt * tile, tile), rows_n), hh * 2 * dh:(hh + 1) * 2 * dh]
                s_new = _dot(k_next, qm_ref[:, sl])
            if cur is not None:
                m_prev = m_ref[:, sl]
                m_new = jnp.maximum(m_prev, mt_ref[:, sl])
                alpha = jnp.exp2(m_prev - m_new)
                p = jnp.exp2(s_ref[:rows_c, sl] - m_new)
                l_ref[:, sl] = alpha * l_ref[:, sl] + jnp.sum(p, axis=0, keepdims=True)
                v_cur = vT_ref[0, cur, hh * dv:(hh + 1) * dv, :rows_c]
                acc_ref[:, sl] = alpha * acc_ref[:, sl] + _dot(v_cur, p.astype(BF16))
                m_ref[:, sl] = m_new
            if nxt is not None:
                if nxt_diag:
                    kpos = lax.broadcasted_iota(jnp.int32, (rows_n, cb), 0)
                    qpos = lax.broadcasted_iota(jnp.int32, (rows_n, cb), 1) + qo
                    s_new = jnp.where(kpos <= qpos, s_new, -jnp.inf)
                s_ref[:rows_n, sl] = s_new
                mt_ref[:, sl] = jnp.max(s_new, axis=0, keepdims=True)

    def finalize(i):
        lam = lam_ref[0, 0]
        inv_l = 1.0 / l_ref[...]
        rows = pl.ds(pl.multiple_of(i * tile, tile), tile)
        for hh in range(hp):
            c1 = slice(hh * 2 * tile, hh * 2 * tile + tile)
            c2 = slice(hh * 2 * tile + tile, (hh + 1) * 2 * tile)
            o = acc_ref[:, c1] * inv_l[:, c1] - lam * (acc_ref[:, c2] * inv_l[:, c2])
            o = o * lax.rsqrt(jnp.mean(o * o, axis=0, keepdims=True) + NORM_EPS) * subln_ref[...]
            o_ref[rows, hh * dv:(hh + 1) * dv] = o.T.astype(o_ref.dtype)

    load_q(0)
    reset()
    step(None, 0, True, False)

    def q_body(i, carry):
        def kv_body(j, c):
            step(j, j + 1, False, False)
            return c

        lax.fori_loop(0, i - 1, kv_body, 0)

        @pl.when(i > 0)
        def _():
            step(i - 1, i, True, False)

        @pl.when(i < nt - 1)
        def _():
            load_q(i + 1)
            step(i, 0, False, True)

        @pl.when(i == nt - 1)
        def _():
            step(i, None, False, True)

        finalize(i)
        reset()
        return carry

    lax.fori_loop(0, nt, q_body, 0)


def _attention(lam, qT, k2d, vT, subln_col, batch, seq, tile):
    nt = seq // tile
    dv = A_V_DIM
    n = batch * seq
    hp = 4
    assert A_HEADS % hp == 0
    kern = functools.partial(_attn_kernel, tile=tile, cb=min(tile, MXU_COLS), hp=hp, nt=nt)
    lanes = hp * 2 * tile
    return pl.pallas_call(
        kern,
        out_shape=jax.ShapeDtypeStruct((n, A_HEADS * dv), BF16),
        grid=(batch, A_HEADS // hp),
        in_specs=[pl.BlockSpec(memory_space=pltpu.SMEM),
                  pl.BlockSpec((1, nt, hp * 2 * A_HEAD_DIM, tile), lambda b, h: (b, 0, h, 0)),
                  pl.BlockSpec((1, seq, hp * 2 * A_HEAD_DIM), lambda b, h: (b, 0, h)),
                  pl.BlockSpec((1, nt, hp * dv, tile), lambda b, h: (b, 0, h, 0)),
                  pl.BlockSpec((dv, 1), lambda b, h: (0, 0))],
        out_specs=pl.BlockSpec((seq, hp * dv), lambda b, h: (b, h)),
        scratch_shapes=[pltpu.VMEM((2 * A_HEAD_DIM, lanes), BF16),
                        pltpu.VMEM((1, lanes), F32),
                        pltpu.VMEM((1, lanes), F32),
                        pltpu.VMEM((dv, lanes), F32),
                        pltpu.VMEM((tile, lanes), F32),
                        pltpu.VMEM((1, lanes), F32)],
        compiler_params=pltpu.CompilerParams(dimension_semantics=("arbitrary",) * 2, vmem_limit_bytes=VMEM_LIMIT),
        name="diff_attention",
    )(lam, qT, k2d.reshape(batch, seq, -1), vT, subln_col)


def _merge_kernel(x_ref, oa_ref, ob_ref, gate_ref, wa_ref, wb_ref, wo_ref, o_ref):
    d = x_ref.shape[1]
    ya = _dot(oa_ref[...], wa_ref[...])
    yb = _dot(ob_ref[...], wb_ref[...])
    g = gate_ref[...].astype(F32)
    merged = _sigmoid(g[:, :d]) * ya + _sigmoid(g[:, d:]) * yb
    o_ref[...] = x_ref[...] + _dot(merged.astype(BF16), wo_ref[...])


def _merge(x2d, oa, ob, gates, wa, wb, wo, tile):
    n, d = x2d.shape
    row = lambda w: pl.BlockSpec((tile, w), lambda i: (i, 0))
    return pl.pallas_call(
        _merge_kernel,
        out_shape=jax.ShapeDtypeStruct((n, d), F32),
        grid=(n // tile,),
        in_specs=[row(d), row(oa.shape[1]), row(ob.shape[1]), row(2 * d),
                  _const_spec(wa.shape), _const_spec(wb.shape), _const_spec(wo.shape)],
        out_specs=row(d),
        compiler_params=pltpu.CompilerParams(dimension_semantics=("arbitrary",), vmem_limit_bytes=VMEM_LIMIT),
        name="merge_outproj",
    )(x2d, oa, ob, gates, wa.astype(BF16), wb.astype(BF16), wo.astype(BF16))


def _ffn_kernel(x_ref, nw_ref, wup_ref, cw_ref, cb_ref, wdown_ref, nf_ref, o_ref, ucar_ref, act_ref, *, tiles_per_seq,
                fchunk, apply_final):
    i = pl.program_id(0)
    tile, d = x_ref.shape
    dff = wdown_ref.shape[0]

    @pl.when(i % tiles_per_seq == 0)
    def _():
        ucar_ref[...] = jnp.zeros(ucar_ref.shape, F32)

    x = x_ref[...]
    h = (x * lax.rsqrt(jnp.mean(x * x, axis=-1, keepdims=True) + NORM_EPS) * nw_ref[...]).astype(BF16)

    def up(c):
        return [_dot(h, wup_ref[:, base + c * fchunk:base + (c + 1) * fchunk]) for base in (0, dff)]

    u_next = up(0)
    for c in range(dff // fchunk):
        u_cur = u_next
        if c + 1 < dff // fchunk:
            u_next = up(c + 1)
        halves = []
        for u, base in zip(u_cur, (0, dff)):
            sl = slice(base + c * fchunk, base + (c + 1) * fchunk)
            uext = jnp.concatenate([ucar_ref[:, sl], u], axis=0)
            cw = cw_ref[:, sl]
            y = cb_ref[:, sl] + cw[FFN_CONV_WIDTH - 1:FFN_CONV_WIDTH] * u
            for s in range(1, FFN_CONV_WIDTH):
                y = y + cw[FFN_CONV_WIDTH - 1 - s:FFN_CONV_WIDTH - s] * pltpu.roll(uext, s, axis=0)[HALO:]
            ucar_ref[:, sl] = u[tile - HALO:, :]
            halves.append(y)
        act_ref[:, c * fchunk:(c + 1) * fchunk] = (_silu(halves[0]) * halves[1]).astype(BF16)
    x2 = x + _dot(act_ref[...], wdown_ref[...])
    if apply_final:
        x2 = x2 * lax.rsqrt(jnp.mean(x2 * x2, axis=-1, keepdims=True) + NORM_EPS) * nf_ref[...]
    o_ref[...] = x2


def _ffn(x2d, nw, wup, cw, cb, wdown, nf, seq, tile, fchunk, apply_final):
    n, d = x2d.shape
    dff = wdown.shape[0]
    row = pl.BlockSpec((tile, d), lambda i: (i, 0))
    kern = functools.partial(_ffn_kernel, tiles_per_seq=seq // tile, fchunk=fchunk, apply_final=apply_final)
    return pl.pallas_call(
        kern,
        out_shape=jax.ShapeDtypeStruct((n, d), F32),
        grid=(n // tile,),
        in_specs=[row, _const_spec((1, d)), _const_spec(wup.shape), _const_spec(cw.shape),
                  _const_spec((1, 2 * dff)), _const_spec(wdown.shape), _const_spec((1, d))],
        out_specs=row,
        scratch_shapes=[pltpu.VMEM((HALO, 2 * dff), F32), pltpu.VMEM((tile, dff), BF16)],
        compiler_params=pltpu.CompilerParams(dimension_semantics=("arbitrary",), vmem_limit_bytes=VMEM_LIMIT),
        name="convffn",
    )(x2d, nw, wup.astype(BF16), cw, cb, wdown.astype(BF16), nf)


def _gdn_kernel(qkv_ref, z_ref, ba_ref, baT_ref, alog_ref, dtb_ref, alogc_ref, dtbc_ref, onorm_ref,
                o_ref, state_ref, *, nb):
    c = GDN_CHUNK
    dk = B_HEAD_DIM
    bw = B_HEADS * dk

    @pl.when(pl.program_id(1) == 0)
    def _():
        state_ref[...] = jnp.zeros(state_ref.shape, F32)

    r = lax.broadcasted_iota(jnp.int32, (c, c), 0)
    q_ = lax.broadcasted_iota(jnp.int32, (c, c), 1)
    lower = r >= q_
    strict = r > q_
    tri_lo = lower.astype(F32)
    tri_up = (r <= q_).astype(F32)
    softplus = lambda t: jnp.maximum(t, 0.0) + jnp.log1p(jnp.exp(-jnp.abs(t)))
    nt = (((1,), (1,)), ((), ()))
    tn = (((0,), (0,)), ((), ()))

    qs, ks, vs, bhs, gchs, glasts, decays = [], [], [], [], [], [], []
    for s_i in range(nb):
        qkv = qkv_ref[s_i].astype(F32)
        ba = ba_ref[s_i]
        baT = baT_ref[s_i, 0]
        beta = _sigmoid(ba)
        g = -jnp.exp(alog_ref[...]) * softplus(ba + dtb_ref[...])
        gT = -jnp.exp(alogc_ref[...]) * softplus(baT + dtbc_ref[...])
        gc = jnp.dot(tri_lo, g, precision=lax.Precision.HIGHEST, preferred_element_type=F32)
        gcT = jnp.dot(gT, tri_up, precision=lax.Precision.HIGHEST, preferred_element_type=F32)
        for h in range(B_HEADS):
            qh = qkv[:, h * dk:(h + 1) * dk]
            kh = qkv[:, bw + h * dk:bw + (h + 1) * dk]
            qs.append(qh * (lax.rsqrt(jnp.sum(qh * qh, axis=-1, keepdims=True) + NORM_EPS) * (dk ** -0.5)))
            ks.append(kh * lax.rsqrt(jnp.sum(kh * kh, axis=-1, keepdims=True) + NORM_EPS))
            vs.append(qkv[:, 2 * bw + h * dk:2 * bw + (h + 1) * dk])
            gl = B_HEADS + h
            bhs.append(beta[:, h:h + 1])
            gchs.append(gc[:, gl:gl + 1])
            glasts.append(gc[c - 1:c, gl:gl + 1])
            diff = gc[:, gl:gl + 1] - gcT[gl:gl + 1, :]
            decays.append(jnp.where(lower, jnp.exp(jnp.where(lower, diff, 0.0)), 0.0))

    chains = range(nb * B_HEADS)
    kbs = [ks[i] * bhs[i] for i in chains]
    kfs = [ks[i].astype(BF16) for i in chains]
    kq = [lax.dot_general(jnp.concatenate([kbs[i], qs[i]], axis=0).astype(BF16), kfs[i], nt,
                          preferred_element_type=F32) for i in chains]
    a_mats = [jnp.where(strict, kq[i][:c] * decays[i], 0.0) for i in chains]
    qks = [jnp.where(lower, kq[i][c:] * decays[i], 0.0).astype(BF16) for i in chains]
    xs = [jnp.concatenate([vs[i] * bhs[i], kbs[i] * jnp.exp(gchs[i])], axis=1) for i in chains]
    pws = [a.astype(BF16) for a in a_mats]
    xs = [xs[i] - _dot(pws[i], xs[i].astype(BF16)) for i in chains]
    for _ in range(int(math.log2(c)) - 1):
        pws = [_dot(p, p).astype(BF16) for p in pws]
        xs = [xs[i] + _dot(pws[i], xs[i].astype(BF16)) for i in chains]
    states = [state_ref[i] for i in chains]
    wq = [jnp.concatenate([xs[i][:, dk:], qs[i] * jnp.exp(gchs[i])], axis=0).astype(BF16) for i in chains]
    ws = [_dot(wq[i], states[i].astype(BF16)) for i in chains]
    v_new = [(xs[i][:, :dk] - ws[i][:c]).astype(BF16) for i in chains]
    kdec = [(ks[i] * jnp.exp(glasts[i] - gchs[i])).astype(BF16) for i in chains]
    for i in chains:
        state_ref[i] = states[i] * jnp.exp(glasts[i]) + lax.dot_general(kdec[i], v_new[i], tn,
                                                                        preferred_element_type=F32)
    outs = [ws[i][c:] + _dot(qks[i], v_new[i]) for i in chains]
    onorm = onorm_ref[...]
    for s_i in range(nb):
        z = z_ref[s_i].astype(F32)
        for h in range(B_HEADS):
            o = outs[s_i * B_HEADS + h]
            o = o * lax.rsqrt(jnp.mean(o * o, axis=-1, keepdims=True) + NORM_EPS) * onorm
            o_ref[s_i, :, h * dk:(h + 1) * dk] = (o * _silu(z[:, h * dk:(h + 1) * dk])).astype(o_ref.dtype)


def _gdn(qkv, z, ba, a_log, dt_bias, onorm, batch, seq):
    c = GDN_CHUNK
    bw = B_HEADS * B_HEAD_DIM
    nb = 2 if batch % 2 == 0 else 1
    nchunk = seq // c
    baT = ba.reshape(batch, nchunk, c, LANES).transpose(0, 1, 3, 2)
    pad = lambda v: jnp.pad(v.astype(F32), (B_HEADS, LANES - 2 * B_HEADS))
    alog, dtb = pad(a_log), pad(dt_bias)
    row = lambda w: pl.BlockSpec((nb, c, w), lambda b, t: (b, t, 0))
    out = pl.pallas_call(
        functools.partial(_gdn_kernel, nb=nb),
        out_shape=jax.ShapeDtypeStruct((batch, seq, bw), BF16),
        grid=(batch // nb, nchunk),
        in_specs=[row(3 * bw), row(bw), row(LANES), pl.BlockSpec((nb, 1, LANES, c), lambda b, t: (b, t, 0, 0)),
                  _const_spec((1, LANES)), _const_spec((1, LANES)),
                  _const_spec((LANES, 1)), _const_spec((LANES, 1)), _const_spec((1, B_HEAD_DIM))],
        out_specs=row(bw),
        scratch_shapes=[pltpu.VMEM((nb * B_HEADS, B_HEAD_DIM, B_HEAD_DIM), F32)],
        compiler_params=pltpu.CompilerParams(dimension_semantics=("arbitrary", "arbitrary"),
                                             vmem_limit_bytes=VMEM_LIMIT),
        name="gated_deltanet",
    )(qkv.reshape(batch, seq, -1), z.reshape(batch, seq, -1), ba.reshape(batch, seq, -1), baT,
      alog.reshape(1, -1), dtb.reshape(1, -1), alog.reshape(-1, 1), dtb.reshape(-1, 1), onorm.reshape(1, -1))
    return out.reshape(batch * seq, bw)


def _lam_kernel(q1_ref, k1_ref, q2_ref, k2_ref, o_ref, *, lambda_init):
    s1 = jnp.sum(q1_ref[...] * k1_ref[...], axis=-1, keepdims=True)
    s2 = jnp.sum(q2_ref[...] * k2_ref[...], axis=-1, keepdims=True)
    o_ref[...] = jnp.exp(s1) - jnp.exp(s2) + lambda_init


def _lambda(q1, k1, q2, k2, lambda_init):
    spec = pl.BlockSpec((1, q1.shape[-1]), lambda: (0, 0))
    return pl.pallas_call(
        functools.partial(_lam_kernel, lambda_init=lambda_init),
        out_shape=jax.ShapeDtypeStruct((1, 1), F32),
        in_specs=[spec] * 4,
        out_specs=pl.BlockSpec((1, 1), lambda: (0, 0)),
        name="lambda",
    )(q1.reshape(1, -1), k1.reshape(1, -1), q2.reshape(1, -1), k2.reshape(1, -1))


def _pick_tile(seq, pref):
    t = min(seq, pref)
    assert seq % t == 0
    return t


def kernel(x, positions, norm_mix, w_in, lambda_q1, lambda_k1, lambda_q2, lambda_k2, a_subln, w_a_out, conv_qkv,
           a_log, dt_bias, b_onorm, w_b_out, w_o, norm_ffn, w_up, ffn_conv, ffn_conv_bias, w_down, norm_final):
    batch, seq, d = x.shape
    depth = w_in.shape[0]
    tile = _pick_tile(seq, 512)
    x2d = x.reshape(batch * seq, d)
    tables = _rope_tables(positions, tile)
    for layer in range(depth):
        lambda_init = 0.8 - 0.6 * math.exp(-0.3 * layer)
        qT, k2d, vT, qkv, z, ba, gates = _inproj(x2d, norm_mix[layer].reshape(1, d), tables, w_in[layer],
                                                 conv_qkv[layer], batch, seq, tile)
        lam = _lambda(lambda_q1[layer], lambda_k1[layer], lambda_q2[layer], lambda_k2[layer], lambda_init)
        subln = (a_subln[layer] * (1.0 - lambda_init)).reshape(-1, 1)
        oa = _attention(lam, qT, k2d, vT, subln, batch, seq, tile)
        ob = _gdn(qkv, z, ba, a_log[layer], dt_bias[layer], b_onorm[layer], batch, seq)
        x2d = _merge(x2d, oa, ob, gates, w_a_out[layer], w_b_out[layer], w_o[layer], tile)
        dff = w_down.shape[1]
        x2d = _ffn(x2d, norm_ffn[layer].reshape(1, d), w_up[layer], ffn_conv[layer],
                   ffn_conv_bias[layer].reshape(1, -1), w_down[layer], norm_final.reshape(1, d), seq, tile,
                   fchunk=256 if dff % 256 == 0 else LANES, apply_final=layer == depth - 1)
    return x2d.reshape(batch, seq, d)
```

```python
import functools
import math

import jax
import jax.numpy as jnp
from jax import lax
from jax.experimental import pallas as pl
from jax.experimental.pallas import tpu as pltpu

NORM_EPS = 1e-6
ROPE_THETA = 10000.0
LANES = 128
MXU_COLS = 256
LOG2E = math.log2(math.e)
VMEM_LIMIT = 56 * 1024 * 1024

A_HEADS = 8
A_HEAD_DIM = 64
A_V_DIM = 128
B_HEADS = 8
B_HEAD_DIM = 128
CONV_WIDTH = 4
FFN_CONV_WIDTH = 3
GDN_CHUNK = 128
HALO = 8
F32 = jnp.float32
BF16 = jnp.bfloat16


def _dot(a, b):
    return jnp.dot(a, b, preferred_element_type=F32)


def _const_spec(shape):
    nd = len(shape)
    return pl.BlockSpec(shape, lambda *_: (0,) * nd, pipeline_mode=pl.Buffered(1))


def _sigmoid(x):
    return 1.0 / (1.0 + jnp.exp(-x))


def _silu(x):
    return x * _sigmoid(x)


def _rope_kernel(pos_ref, freq_ref, cos_ref, sin_ref):
    ang = freq_ref[...] * pos_ref[...].astype(F32)
    cos_ref[...] = jnp.cos(ang)
    sin_ref[...] = jnp.sin(ang)


def _rope_tables(positions, tile):
    n = positions.size
    half = A_HEAD_DIM // 2
    inv_freq = ROPE_THETA ** (-jnp.arange(0, A_HEAD_DIM, 2, dtype=F32) / A_HEAD_DIM)
    col = pl.BlockSpec((half, tile), lambda i: (0, i))
    return pl.pallas_call(
        _rope_kernel,
        out_shape=(jax.ShapeDtypeStruct((half, n), F32),) * 2,
        grid=(n // tile,),
        in_specs=[pl.BlockSpec((1, tile), lambda i: (0, i)), pl.BlockSpec((half, 1), lambda i: (0, 0))],
        out_specs=(col, col),
        name="rope_tables",
    )(positions.reshape(1, n), inv_freq.reshape(half, 1))


def _rotary_t(y, cos_t, sin_t, scale):
    half = A_HEAD_DIM // 2
    outs = []
    for g in range(y.shape[0] // A_HEAD_DIM):
        x1 = y[g * A_HEAD_DIM:g * A_HEAD_DIM + half]
        x2 = y[g * A_HEAD_DIM + half:(g + 1) * A_HEAD_DIM]
        outs.append((x1 * cos_t - x2 * sin_t) * scale)
        outs.append((x2 * cos_t + x1 * sin_t) * scale)
    return jnp.concatenate(outs, axis=0)


def _dot_nt(a, b):
    return lax.dot_general(a, b, (((1,), (1,)), ((), ())), preferred_element_type=F32)


def _inproj_kernel(x_ref, nw_ref, cosT_ref, sinT_ref, wqT_ref, wkT_ref, wvT_ref, wpre_ref, wz_ref, wba_ref,
                   wgate_ref, cw_ref, qT_ref, k_ref, vT_ref, qkv_ref, z_ref, ba_ref, gate_ref, carry_ref, *,
                   tiles_per_seq, chunk):
    tile = x_ref.shape[0]

    @pl.when(pl.program_id(0) % tiles_per_seq == 0)
    def _():
        carry_ref[...] = jnp.zeros(carry_ref.shape, F32)

    x = x_ref[...]
    h = (x * lax.rsqrt(jnp.mean(x * x, axis=-1, keepdims=True) + NORM_EPS) * nw_ref[...]).astype(BF16)
    ba_ref[...] = _dot(h, wba_ref[...])

    cos_t, sin_t = cosT_ref[...], sinT_ref[...]

    def q_job(sl):
        def fin(u):
            qT_ref[0, 0, sl, :] = _rotary_t(u, cos_t, sin_t, A_HEAD_DIM ** -0.5 * LOG2E).astype(BF16)
        return (lambda: _dot_nt(wqT_ref[sl, :], h)), fin

    def v_job(sl):
        def fin(u):
            vT_ref[0, 0, sl, :] = u.astype(BF16)
        return (lambda: _dot_nt(wvT_ref[sl, :], h)), fin

    def k_job(sl):
        def fin(u):
            k_ref[:, sl] = _rotary_t(u, cos_t, sin_t, 1.0).T.astype(BF16)
        return (lambda: _dot_nt(wkT_ref[sl, :], h)), fin

    def plain_job(w_ref, o_ref, sl):
        def fin(u):
            o_ref[:, sl] = u.astype(o_ref.dtype)
        return (lambda: _dot(h, w_ref[:, sl])), fin

    def conv_job(sl):
        def fin(u):
            ext = jnp.concatenate([carry_ref[:, sl], u], axis=0)
            cw = cw_ref[:, sl]
            y = cw[CONV_WIDTH - 1:CONV_WIDTH] * u
            for s in range(1, CONV_WIDTH):
                y = y + cw[CONV_WIDTH - 1 - s:CONV_WIDTH - s] * pltpu.roll(ext, s, axis=0)[HALO:]
            carry_ref[:, sl] = u[tile - HALO:, :]
            qkv_ref[:, sl] = _silu(y).astype(qkv_ref.dtype)
        return (lambda: _dot(h, wpre_ref[:, sl])), fin

    chunks = lambda width: [slice(c * chunk, (c + 1) * chunk) for c in range(width // chunk)]
    light = ([q_job(sl) for sl in chunks(wqT_ref.shape[0])] + [k_job(sl) for sl in chunks(wkT_ref.shape[0])]
             + [v_job(sl) for sl in chunks(wvT_ref.shape[0])]
             + [plain_job(wz_ref, z_ref, sl) for sl in chunks(wz_ref.shape[1])]
             + [plain_job(wgate_ref, gate_ref, sl) for sl in chunks(wgate_ref.shape[1])])
    heavy = [conv_job(sl) for sl in chunks(wpre_ref.shape[1])]
    per = max(1, len(light) // len(heavy))
    jobs = []
    for n, hv in enumerate(heavy):
        jobs += light[n * per:(n + 1) * per] + [hv]
    jobs += light[len(heavy) * per:]
    u_next = jobs[0][0]()
    for n, (_, fin) in enumerate(jobs):
        u = u_next
        if n + 1 < len(jobs):
            u_next = jobs[n + 1][0]()
        fin(u)


def _inproj(x2d, nw, tables, w_in, conv_w, batch, seq, tile):
    n, d = x2d.shape
    nt = seq // tile
    aw = A_HEADS * 2 * A_HEAD_DIM
    bw = B_HEADS * B_HEAD_DIM
    half = A_HEAD_DIM // 2
    cos_t, sin_t = tables
    wb = w_in.astype(BF16)
    o = 0
    wq_t = wb[:, o:o + aw].T; o += aw
    wk_t = wb[:, o:o + aw].T; o += aw
    wv_t = wb[:, o:o + aw].T; o += aw
    wpre = wb[:, o:o + 3 * bw]; o += 3 * bw
    wz = wb[:, o:o + bw]; o += bw
    wba = jnp.pad(wb[:, o:o + 2 * B_HEADS], ((0, 0), (0, LANES - 2 * B_HEADS))); o += 2 * B_HEADS
    wgate = wb[:, o:o + 2 * d]
    row = lambda w: pl.BlockSpec((tile, w), lambda i: (i, 0))
    col = pl.BlockSpec((half, tile), lambda i: (0, i))
    tspec = pl.BlockSpec((1, 1, aw, tile), lambda i: (i // nt, i % nt, 0, 0))
    chunk = MXU_COLS if bw % MXU_COLS == 0 and d % MXU_COLS == 0 else LANES
    return pl.pallas_call(
        functools.partial(_inproj_kernel, tiles_per_seq=nt, chunk=chunk),
        out_shape=(jax.ShapeDtypeStruct((batch, nt, aw, tile), BF16),
                   jax.ShapeDtypeStruct((n, aw), BF16),
                   jax.ShapeDtypeStruct((batch, nt, aw, tile), BF16),
                   jax.ShapeDtypeStruct((n, 3 * bw), BF16),
                   jax.ShapeDtypeStruct((n, bw), BF16),
                   jax.ShapeDtypeStruct((n, LANES), F32),
                   jax.ShapeDtypeStruct((n, 2 * d), BF16)),
        grid=(n // tile,),
        in_specs=[row(d), _const_spec((1, d)), col, col,
                  _const_spec(wq_t.shape), _const_spec(wk_t.shape), _const_spec(wv_t.shape),
                  _const_spec(wpre.shape), _const_spec(wz.shape), _const_spec(wba.shape),
                  _const_spec(wgate.shape), _const_spec(conv_w.shape)],
        out_specs=(tspec, row(aw), tspec, row(3 * bw), row(bw), row(LANES), row(2 * d)),
        scratch_shapes=[pltpu.VMEM((HALO, 3 * bw), F32)],
        compiler_params=pltpu.CompilerParams(dimension_semantics=("arbitrary",), vmem_limit_bytes=VMEM_LIMIT),
        name="inproj",
    )(x2d, nw, cos_t, sin_t, wq_t, wk_t, wv_t, wpre, wz, wba, wgate, conv_w)


def _attn_kernel(lam_ref, qT_ref, k_ref, vT_ref, subln_ref, o_ref, qm_ref, m_ref, l_ref, acc_ref, s_ref, mt_ref, *,
                 tile, cb, hp, nt):
    dh = A_HEAD_DIM
    dv = A_V_DIM
    ncb = 2 * tile // cb
    zero = jnp.zeros((dh, tile), BF16)

    def load_q(i):
        for hh in range(hp):
            qT = qT_ref[0, i, hh * 2 * dh:(hh + 1) * 2 * dh, :]
            qm_ref[:, hh * 2 * tile:hh * 2 * tile + tile] = jnp.concatenate([qT[:dh], zero], axis=0)
            qm_ref[:, hh * 2 * tile + tile:(hh + 1) * 2 * tile] = jnp.concatenate([zero, qT[dh:]], axis=0)

    def reset():
        m_ref[...] = jnp.full(m_ref.shape, -jnp.inf, F32)
        l_ref[...] = jnp.zeros(l_ref.shape, F32)
        acc_ref[...] = jnp.zeros(acc_ref.shape, F32)

    def step(cur, nxt, nxt_diag, cur_diag):
        for c in range(hp * ncb):
            hh = c // ncb
            sl = slice(c * cb, (c + 1) * cb)
            qo = (c * cb) % tile
            rows_n = min(tile, qo + cb) if nxt_diag else tile
            rows_c = min(tile, qo + cb) if cur_diag else tile
            if nxt is not None:
                k_next = k_ref[0, pl.ds(pl.multiple_of(nxt * tile, tile), rows_n), hh * 2 * dh:(hh + 1) * 2 * dh]
                s_new = _dot(k_next, qm_ref[:, sl])
            if cur is not None:
                m_prev = m_ref[:, sl]
                m_new = jnp.maximum(m_prev, mt_ref[:, sl])
                alpha = jnp.exp2(m_prev - m_new)
                p = jnp.exp2(s_ref[:rows_c, sl] - m_new)
                l_ref[:, sl] = alpha * l_ref[:, sl] + jnp.sum(p, axis=0, keepdims=True)
                v_cur = vT_ref[0, cur, hh * dv:(hh + 1) * dv, :rows_c]
                acc_ref[:, sl] = alpha * acc_ref[:, sl] + _dot(v_cur, p.astype(BF16))
                m_ref[:, sl] = m_new
            if nxt is not None:
                if nxt_diag:
                    kpos = lax.broadcasted_iota(jnp.int32, (rows_n, cb), 0)
                    qpos = lax.broadcasted_iota(jnp.int32, (rows_n, cb), 1) + qo
                    s_new = jnp.where(kpos <= qpos, s_new, -jnp.inf)
                s_ref[:rows_n, sl] = s_new
                mt_ref[:, sl] = jnp.max(s_new, axis=0, keepdims=True)

    def finalize(i):
        lam = lam_ref[0, 0]
        inv_l = 1.0 / l_ref[...]
        rows = pl.ds(pl.multiple_of(i * tile, tile), tile)
        for hh in range(hp):
            c1 = slice(hh * 2 * tile, hh * 2 * tile + tile)
            c2 = slice(hh * 2 * tile + tile, (hh + 1) * 2 * tile)
            o = acc_ref[:, c1] * inv_l[:, c1] - lam * (acc_ref[:, c2] * inv_l[:, c2])
            o = o * lax.rsqrt(jnp.mean(o * o, axis=0, keepdims=True) + NORM_EPS) * subln_ref[...]
            o_ref[rows, hh * dv:(hh + 1) * dv] = o.T.astype(o_ref.dtype)

    load_q(0)
    reset()
    step(None, 0, True, False)

    def q_body(i, carry):
        def kv_body(j, c):
            step(j, j + 1, False, False)
            return c

        lax.fori_loop(0, i - 1, kv_body, 0)

        @pl.when(i > 0)
        def _():
            step(i - 1, i, True, False)

        @pl.when(i < nt - 1)
        def _():
            load_q(i + 1)
            step(i, 0, False, True)

        @pl.when(i == nt - 1)
        def _():
            step(i, None, False, True)

        finalize(i)
        reset()
        return carry

    lax.fori_loop(0, nt, q_body, 0)


def _attention(lam, qT, k2d, vT, subln_col, batch, seq, tile):
    nt = seq // tile
    dv = A_V_DIM
    n = batch * seq
    hp = 4
    assert A_HEADS % hp == 0
    kern = functools.partial(_attn_kernel, tile=tile, cb=min(tile, MXU_COLS), hp=hp, nt=nt)
    lanes = hp * 2 * tile
    return pl.pallas_call(
        kern,
        out_shape=jax.ShapeDtypeStruct((n, A_HEADS * dv), BF16),
        grid=(batch, A_HEADS // hp),
        in_specs=[pl.BlockSpec(memory_space=pltpu.SMEM),
                  pl.BlockSpec((1, nt, hp * 2 * A_HEAD_DIM, tile), lambda b, h: (b, 0, h, 0)),
                  pl.BlockSpec((1, seq, hp * 2 * A_HEAD_DIM), lambda b, h: (b, 0, h)),
                  pl.BlockSpec((1, nt, hp * dv, tile), lambda b, h: (b, 0, h, 0)),
                  pl.BlockSpec((dv, 1), lambda b, h: (0, 0))],
        out_specs=pl.BlockSpec((seq, hp * dv), lambda b, h: (b, h)),
        scratch_shapes=[pltpu.VMEM((2 * A_HEAD_DIM, lanes), BF16),
                        pltpu.VMEM((1, lanes), F32),
                        pltpu.VMEM((1, lanes), F32),
                        pltpu.VMEM((dv, lanes), F32),
                        pltpu.VMEM((tile, lanes), F32),
                        pltpu.VMEM((1, lanes), F32)],
        compiler_params=pltpu.CompilerParams(dimension_semantics=("arbitrary",) * 2, vmem_limit_bytes=VMEM_LIMIT),
        name="diff_attention",
    )(lam, qT, k2d.reshape(batch, seq, -1), vT, subln_col)


def _merge_kernel(x_ref, oa_ref, ob_ref, gate_ref, wa_ref, wb_ref, wo_ref, o_ref):
    d = x_ref.shape[1]
    ya = _dot(oa_ref[...], wa_ref[...])
    yb = _dot(ob_ref[...], wb_ref[...])
    g = gate_ref[...].astype(F32)
    merged = _sigmoid(g[:, :d]) * ya + _sigmoid(g[:, d:]) * yb
    o_ref[...] = x_ref[...] + _dot(merged.astype(BF16), wo_ref[...])


def _merge(x2d, oa, ob, gates, wa, wb, wo, tile):
    n, d = x2d.shape
    row = lambda w: pl.BlockSpec((tile, w), lambda i: (i, 0))
    return pl.pallas_call(
        _merge_kernel,
        out_shape=jax.ShapeDtypeStruct((n, d), F32),
        grid=(n // tile,),
        in_specs=[row(d), row(oa.shape[1]), row(ob.shape[1]), row(2 * d),
                  _const_spec(wa.shape), _const_spec(wb.shape), _const_spec(wo.shape)],
        out_specs=row(d),
        compiler_params=pltpu.CompilerParams(dimension_semantics=("arbitrary",), vmem_limit_bytes=VMEM_LIMIT),
        name="merge_outproj",
    )(x2d, oa, ob, gates, wa.astype(BF16), wb.astype(BF16), wo.astype(BF16))


def _ffn_kernel(x_ref, nw_ref, wup_ref, cw_ref, cb_ref, wdown_ref, nf_ref, o_ref, ucar_ref, act_ref, *, tiles_per_seq,
                fchunk, apply_final):
    i = pl.program_id(0)
    tile, d = x_ref.shape
    dff = wdown_ref.shape[0]

    @pl.when(i % tiles_per_seq == 0)
    def _():
        ucar_ref[...] = jnp.zeros(ucar_ref.shape, F32)

    x = x_ref[...]
    h = (x * lax.rsqrt(jnp.mean(x * x, axis=-1, keepdims=True) + NORM_EPS) * nw_ref[...]).astype(BF16)

    def up(c):
        return [_dot(h, wup_ref[:, base + c * fchunk:base + (c + 1) * fchunk]) for base in (0, dff)]

    u_next = up(0)
    for c in range(dff // fchunk):
        u_cur = u_next
        if c + 1 < dff // fchunk:
            u_next = up(c + 1)
        halves = []
        for u, base in zip(u_cur, (0, dff)):
            sl = slice(base + c * fchunk, base + (c + 1) * fchunk)
            uext = jnp.concatenate([ucar_ref[:, sl], u], axis=0)
            cw = cw_ref[:, sl]
            y = cb_ref[:, sl] + cw[FFN_CONV_WIDTH - 1:FFN_CONV_WIDTH] * u
            for s in range(1, FFN_CONV_WIDTH):
                y = y + cw[FFN_CONV_WIDTH - 1 - s:FFN_CONV_WIDTH - s] * pltpu.roll(uext, s, axis=0)[HALO:]
            ucar_ref[:, sl] = u[tile - HALO:, :]
            halves.append(y)
        act_ref[:, c * fchunk:(c + 1) * fchunk] = (_silu(halves[0]) * halves[1]).astype(BF16)
    x2 = x + _dot(act_ref[...], wdown_ref[...])
    if apply_final:
        x2 = x2 * lax.rsqrt(jnp.mean(x2 * x2, axis=-1, keepdims=True) + NORM_EPS) * nf_ref[...]
    o_ref[...] = x2


def _ffn(x2d, nw, wup, cw, cb, wdown, nf, seq, tile, fchunk, apply_final):
    n, d = x2d.shape
    dff = wdown.shape[0]
    row = pl.BlockSpec((tile, d), lambda i: (i, 0))
    kern = functools.partial(_ffn_kernel, tiles_per_seq=seq // tile, fchunk=fchunk, apply_final=apply_final)
    return pl.pallas_call(
        kern,
        out_shape=jax.ShapeDtypeStruct((n, d), F32),
        grid=(n // tile,),
        in_specs=[row, _const_spec((1, d)), _const_spec(wup.shape), _const_spec(cw.shape),
                  _const_spec((1, 2 * dff)), _const_spec(wdown.shape), _const_spec((1, d))],
        out_specs=row,
        scratch_shapes=[pltpu.VMEM((HALO, 2 * dff), F32), pltpu.VMEM((tile, dff), BF16)],
        compiler_params=pltpu.CompilerParams(dimension_semantics=("arbitrary",), vmem_limit_bytes=VMEM_LIMIT),
        name="convffn",
    )(x2d, nw, wup.astype(BF16), cw, cb, wdown.astype(BF16), nf)


def _gdn_kernel(qkv_ref, z_ref, ba_ref, baT_ref, alog_ref, dtb_ref, alogc_ref, dtbc_ref, onorm_ref, o_ref,
                state_ref, lhs_ref, kf_ref, dec_ref, x0_ref, qeg_ref, kdec_ref, egl_ref, *, nb):
    c = GDN_CHUNK
    dk = B_HEAD_DIM
    bw = B_HEADS * dk
    chains = range(nb * B_HEADS)
    prepared = (lhs_ref, kf_ref, dec_ref, x0_ref, qeg_ref, kdec_ref, egl_ref)

    @pl.when(pl.program_id(1) == 0)
    def _():
        state_ref[...] = jnp.zeros(state_ref.shape, F32)
        for ref in prepared:
            ref[...] = jnp.zeros(ref.shape, ref.dtype)

    r = lax.broadcasted_iota(jnp.int32, (c, c), 0)
    q_ = lax.broadcasted_iota(jnp.int32, (c, c), 1)
    lower = r >= q_
    strict = r > q_
    tri_lo = lower.astype(F32)
    tri_up = (r <= q_).astype(F32)
    softplus = lambda t: jnp.maximum(t, 0.0) + jnp.log1p(jnp.exp(-jnp.abs(t)))
    nt = (((1,), (1,)), ((), ()))
    tn = (((0,), (0,)), ((), ()))

    lhs = [lhs_ref[i] for i in chains]
    kfs = [kf_ref[i] for i in chains]
    decays = [dec_ref[i] for i in chains]
    xs = [x0_ref[i] for i in chains]
    qeg = [qeg_ref[i] for i in chains]
    kdec = [kdec_ref[i] for i in chains]
    egl = [egl_ref[i][0:1, :] for i in chains]

    for s_i in range(nb):
        qkv = qkv_ref[s_i].astype(F32)
        ba = ba_ref[s_i]
        baT = baT_ref[s_i, 0]
        beta = _sigmoid(ba)
        g = -jnp.exp(alog_ref[...]) * softplus(ba + dtb_ref[...])
        gT = -jnp.exp(alogc_ref[...]) * softplus(baT + dtbc_ref[...])
        gc = jnp.dot(tri_lo, g, precision=lax.Precision.HIGHEST, preferred_element_type=F32)
        gcT = jnp.dot(gT, tri_up, precision=lax.Precision.HIGHEST, preferred_element_type=F32)
        for h in range(B_HEADS):
            i = s_i * B_HEADS + h
            qh = qkv[:, h * dk:(h + 1) * dk]
            kh = qkv[:, bw + h * dk:bw + (h + 1) * dk]
            vh = qkv[:, 2 * bw + h * dk:2 * bw + (h + 1) * dk]
            qn = qh * (lax.rsqrt(jnp.sum(qh * qh, axis=-1, keepdims=True) + NORM_EPS) * (dk ** -0.5))
            kn = kh * lax.rsqrt(jnp.sum(kh * kh, axis=-1, keepdims=True) + NORM_EPS)
            gl = B_HEADS + h
            bh = beta[:, h:h + 1]
            gch = gc[:, gl:gl + 1]
            glast = gc[c - 1:c, gl:gl + 1]
            diff = gch - gcT[gl:gl + 1, :]
            eg = jnp.exp(gch)
            kb = kn * bh
            lhs_ref[i] = jnp.concatenate([kb, qn], axis=0).astype(BF16)
            kf_ref[i] = kn.astype(BF16)
            dec_ref[i] = jnp.where(lower, jnp.exp(jnp.where(lower, diff, 0.0)), 0.0)
            x0_ref[i] = jnp.concatenate([vh * bh, kb * eg], axis=1)
            qeg_ref[i] = (qn * eg).astype(BF16)
            kdec_ref[i] = (kn * jnp.exp(glast - gch)).astype(BF16)
            egl_ref[i] = jnp.broadcast_to(jnp.exp(glast), egl_ref.shape[1:])

    kq = [lax.dot_general(lhs[i], kfs[i], nt, preferred_element_type=F32) for i in chains]
    pws = [jnp.where(strict, kq[i][:c] * decays[i], 0.0).astype(BF16) for i in chains]
    qks = [jnp.where(lower, kq[i][c:] * decays[i], 0.0).astype(BF16) for i in chains]
    xs = [xs[i] - _dot(pws[i], xs[i].astype(BF16)) for i in chains]
    for _ in range(int(math.log2(c)) - 1):
        pws = [_dot(p, p).astype(BF16) for p in pws]
        xs = [xs[i] + _dot(pws[i], xs[i].astype(BF16)) for i in chains]
    states = [state_ref[i] for i in chains]
    wq = [jnp.concatenate([xs[i][:, dk:].astype(BF16), qeg[i]], axis=0) for i in chains]
    ws = [_dot(wq[i], states[i].astype(BF16)) for i in chains]
    v_new = [(xs[i][:, :dk] - ws[i][:c]).astype(BF16) for i in chains]
    for i in chains:
        state_ref[i] = states[i] * egl[i] + lax.dot_general(kdec[i], v_new[i], tn, preferred_element_type=F32)
    outs = [ws[i][c:] + _dot(qks[i], v_new[i]) for i in chains]
    onorm = onorm_ref[...]
    for s_i in range(nb):
        z = z_ref[s_i].astype(F32)
        for h in range(B_HEADS):
            o = outs[s_i * B_HEADS + h]
            o = o * lax.rsqrt(jnp.mean(o * o, axis=-1, keepdims=True) + NORM_EPS) * onorm
            o_ref[s_i, :, h * dk:(h + 1) * dk] = (o * _silu(z[:, h * dk:(h + 1) * dk])).astype(o_ref.dtype)


def _gdn(qkv, z, ba, a_log, dt_bias, onorm, batch, seq):
    c = GDN_CHUNK
    dk = B_HEAD_DIM
    bw = B_HEADS * dk
    nb = 2 if batch % 2 == 0 else 1
    nchunk = seq // c
    nch = nb * B_HEADS
    baT = ba.reshape(batch, nchunk, c, LANES).transpose(0, 1, 3, 2)
    pad = lambda v: jnp.pad(v.astype(F32), (B_HEADS, LANES - 2 * B_HEADS))
    alog, dtb = pad(a_log), pad(dt_bias)
    prep = lambda w: pl.BlockSpec((nb, c, w), lambda b, t: (b, jnp.minimum(t, nchunk - 1), 0))
    done = lambda w: pl.BlockSpec((nb, c, w), lambda b, t: (b, jnp.maximum(t - 1, 0), 0))
    out = pl.pallas_call(
        functools.partial(_gdn_kernel, nb=nb),
        out_shape=jax.ShapeDtypeStruct((batch, seq, bw), BF16),
        grid=(batch // nb, nchunk + 1),
        in_specs=[prep(3 * bw), done(bw), prep(LANES),
                  pl.BlockSpec((nb, 1, LANES, c), lambda b, t: (b, jnp.minimum(t, nchunk - 1), 0, 0)),
                  _const_spec((1, LANES)), _const_spec((1, LANES)),
                  _const_spec((LANES, 1)), _const_spec((LANES, 1)), _const_spec((1, dk))],
        out_specs=done(bw),
        scratch_shapes=[pltpu.VMEM((nch, dk, dk), F32),
                        pltpu.VMEM((nch, 2 * c, dk), BF16),
                        pltpu.VMEM((nch, c, dk), BF16),
                        pltpu.VMEM((nch, c, c), F32),
                        pltpu.VMEM((nch, c, 2 * dk), F32),
                        pltpu.VMEM((nch, c, dk), BF16),
                        pltpu.VMEM((nch, c, dk), BF16),
                        pltpu.VMEM((nch, HALO, LANES), F32)],
        compiler_params=pltpu.CompilerParams(dimension_semantics=("arbitrary", "arbitrary"),
                                             vmem_limit_bytes=VMEM_LIMIT),
        name="gated_deltanet",
    )(qkv.reshape(batch, seq, -1), z.reshape(batch, seq, -1), ba.reshape(batch, seq, -1), baT,
      alog.reshape(1, -1), dtb.reshape(1, -1), alog.reshape(-1, 1), dtb.reshape(-1, 1), onorm.reshape(1, -1))
    return out.reshape(batch * seq, bw)


def _lam_kernel(q1_ref, k1_ref, q2_ref, k2_ref, o_ref, *, lambda_init):
    s1 = jnp.sum(q1_ref[...] * k1_ref[...], axis=-1, keepdims=True)
    s2 = jnp.sum(q2_ref[...] * k2_ref[...], axis=-1, keepdims=True)
    o_ref[...] = jnp.exp(s1) - jnp.exp(s2) + lambda_init


def _lambda(q1, k1, q2, k2, lambda_init):
    spec = pl.BlockSpec((1, q1.shape[-1]), lambda: (0, 0))
    return pl.pallas_call(
        functools.partial(_lam_kernel, lambda_init=lambda_init),
        out_shape=jax.ShapeDtypeStruct((1, 1), F32),
        in_specs=[spec] * 4,
        out_specs=pl.BlockSpec((1, 1), lambda: (0, 0)),
        name="lambda",
    )(q1.reshape(1, -1), k1.reshape(1, -1), q2.reshape(1, -1), k2.reshape(1, -1))


def _pick_tile(seq, pref):
    t = min(seq, pref)
    assert seq % t == 0
    return t


def kernel(x, positions, norm_mix, w_in, lambda_q1, lambda_k1, lambda_q2, lambda_k2, a_subln, w_a_out, conv_qkv,
           a_log, dt_bias, b_onorm, w_b_out, w_o, norm_ffn, w_up, ffn_conv, ffn_conv_bias, w_down, norm_final):
    batch, seq, d = x.shape
    depth = w_in.shape[0]
    tile = _pick_tile(seq, 512)
    x2d = x.reshape(batch * seq, d)
    tables = _rope_tables(positions, tile)
    for layer in range(depth):
        lambda_init = 0.8 - 0.6 * math.exp(-0.3 * layer)
        qT, k2d, vT, qkv, z, ba, gates = _inproj(x2d, norm_mix[layer].reshape(1, d), tables, w_in[layer],
                                                 conv_qkv[layer], batch, seq, tile)
        lam = _lambda(lambda_q1[layer], lambda_k1[layer], lambda_q2[layer], lambda_k2[layer], lambda_init)
        subln = (a_subln[layer] * (1.0 - lambda_init)).reshape(-1, 1)
        oa = _attention(lam, qT, k2d, vT, subln, batch, seq, tile)
        ob = _gdn(qkv, z, ba, a_log[layer], dt_bias[layer], b_onorm[layer], batch, seq)
        x2d = _merge(x2d, oa, ob, gates, w_a_out[layer], w_b_out[layer], w_o[layer], tile)
        dff = w_down.shape[1]
        x2d = _ffn(x2d, norm_ffn[layer].reshape(1, d), w_up[layer], ffn_conv[layer],
                   ffn_conv_bias[layer].reshape(1, -1), w_down[layer], norm_final.reshape(1, d), seq, tile,
                   fchunk=256 if dff % 256 == 0 else LANES, apply_final=layer == depth - 1)
    return x2d.reshape(batch, seq, d)
```

```python
import functools
import math

import jax
import jax.numpy as jnp
from jax import lax
from jax.experimental import pallas as pl
from jax.experimental.pallas import tpu as pltpu

NORM_EPS = 1e-6
ROPE_THETA = 10000.0
LANES = 128
MXU_COLS = 256
LOG2E = math.log2(math.e)
VMEM_LIMIT = 56 * 1024 * 1024

A_HEADS = 8
A_HEAD_DIM = 64
A_V_DIM = 128
B_HEADS = 8
B_HEAD_DIM = 128
CONV_WIDTH = 4
FFN_CONV_WIDTH = 3
GDN_CHUNK = 128
SOLVE_BASE = 8
HALO = 8
F32 = jnp.float32
BF16 = jnp.bfloat16


def _dot(a, b):
    return jnp.dot(a, b, preferred_element_type=F32)


def _const_spec(shape):
    nd = len(shape)
    return pl.BlockSpec(shape, lambda *_: (0,) * nd, pipeline_mode=pl.Buffered(1))


def _sigmoid(x):
    return 1.0 / (1.0 + jnp.exp(-x))


def _silu(x):
    return x * _sigmoid(x)


def _rope_kernel(pos_ref, freq_ref, cos_ref, sin_ref):
    ang = freq_ref[...] * pos_ref[...].astype(F32)
    cos_ref[...] = jnp.cos(ang)
    sin_ref[...] = jnp.sin(ang)


def _rope_tables(positions, tile):
    n = positions.size
    half = A_HEAD_DIM // 2
    inv_freq = ROPE_THETA ** (-jnp.arange(0, A_HEAD_DIM, 2, dtype=F32) / A_HEAD_DIM)
    col = pl.BlockSpec((half, tile), lambda i: (0, i))
    return pl.pallas_call(
        _rope_kernel,
        out_shape=(jax.ShapeDtypeStruct((half, n), F32),) * 2,
        grid=(n // tile,),
        in_specs=[pl.BlockSpec((1, tile), lambda i: (0, i)), pl.BlockSpec((half, 1), lambda i: (0, 0))],
        out_specs=(col, col),
        name="rope_tables",
    )(positions.reshape(1, n), inv_freq.reshape(half, 1))


def _rotary_t(y, cos_t, sin_t, scale):
    half = A_HEAD_DIM // 2
    outs = []
    for g in range(y.shape[0] // A_HEAD_DIM):
        x1 = y[g * A_HEAD_DIM:g * A_HEAD_DIM + half]
        x2 = y[g * A_HEAD_DIM + half:(g + 1) * A_HEAD_DIM]
        outs.append((x1 * cos_t - x2 * sin_t) * scale)
        outs.append((x2 * cos_t + x1 * sin_t) * scale)
    return jnp.concatenate(outs, axis=0)


def _dot_nt(a, b):
    return lax.dot_general(a, b, (((1,), (1,)), ((), ())), preferred_element_type=F32)


def _inproj_kernel(x_ref, nw_ref, cosT_ref, sinT_ref, wqT_ref, wkT_ref, wvT_ref, wpre_ref, wz_ref, wba_ref,
                   wgate_ref, cw_ref, qT_ref, k_ref, vT_ref, qkv_ref, z_ref, ba_ref, gate_ref, carry_ref, *,
                   tiles_per_seq, chunk):
    tile = x_ref.shape[0]

    @pl.when(pl.program_id(0) % tiles_per_seq == 0)
    def _():
        carry_ref[...] = jnp.zeros(carry_ref.shape, F32)

    x = x_ref[...]
    h = (x * lax.rsqrt(jnp.mean(x * x, axis=-1, keepdims=True) + NORM_EPS) * nw_ref[...]).astype(BF16)
    ba_ref[...] = _dot(h, wba_ref[...])

    cos_t, sin_t = cosT_ref[...], sinT_ref[...]

    def q_job(sl):
        def fin(u):
            qT_ref[0, 0, sl, :] = _rotary_t(u, cos_t, sin_t, A_HEAD_DIM ** -0.5 * LOG2E).astype(BF16)
        return (lambda: _dot_nt(wqT_ref[sl, :], h)), fin

    def v_job(sl):
        def fin(u):
            vT_ref[0, 0, sl, :] = u.astype(BF16)
        return (lambda: _dot_nt(wvT_ref[sl, :], h)), fin

    def k_job(sl):
        def fin(u):
            k_ref[:, sl] = _rotary_t(u, cos_t, sin_t, 1.0).T.astype(BF16)
        return (lambda: _dot_nt(wkT_ref[sl, :], h)), fin

    def plain_job(w_ref, o_ref, sl):
        def fin(u):
            o_ref[:, sl] = u.astype(o_ref.dtype)
        return (lambda: _dot(h, w_ref[:, sl])), fin

    def conv_job(sl):
        def fin(u):
            ext = jnp.concatenate([carry_ref[:, sl], u], axis=0)
            cw = cw_ref[:, sl]
            y = cw[CONV_WIDTH - 1:CONV_WIDTH] * u
            for s in range(1, CONV_WIDTH):
                y = y + cw[CONV_WIDTH - 1 - s:CONV_WIDTH - s] * pltpu.roll(ext, s, axis=0)[HALO:]
            carry_ref[:, sl] = u[tile - HALO:, :]
            qkv_ref[:, sl] = _silu(y).astype(qkv_ref.dtype)
        return (lambda: _dot(h, wpre_ref[:, sl])), fin

    chunks = lambda width: [slice(c * chunk, (c + 1) * chunk) for c in range(width // chunk)]
    light = ([q_job(sl) for sl in chunks(wqT_ref.shape[0])] + [k_job(sl) for sl in chunks(wkT_ref.shape[0])]
             + [v_job(sl) for sl in chunks(wvT_ref.shape[0])]
             + [plain_job(wz_ref, z_ref, sl) for sl in chunks(wz_ref.shape[1])]
             + [plain_job(wgate_ref, gate_ref, sl) for sl in chunks(wgate_ref.shape[1])])
    heavy = [conv_job(sl) for sl in chunks(wpre_ref.shape[1])]
    per = max(1, len(light) // len(heavy))
    jobs = []
    for n, hv in enumerate(heavy):
        jobs += light[n * per:(n + 1) * per] + [hv]
    jobs += light[len(heavy) * per:]
    u_next = jobs[0][0]()
    for n, (_, fin) in enumerate(jobs):
        u = u_next
        if n + 1 < len(jobs):
            u_next = jobs[n + 1][0]()
        fin(u)


def _inproj(x2d, nw, tables, w_in, conv_w, batch, seq, tile):
    n, d = x2d.shape
    nt = seq // tile
    aw = A_HEADS * 2 * A_HEAD_DIM
    bw = B_HEADS * B_HEAD_DIM
    half = A_HEAD_DIM // 2
    cos_t, sin_t = tables
    wb = w_in.astype(BF16)
    o = 0
    wq_t = wb[:, o:o + aw].T; o += aw
    wk_t = wb[:, o:o + aw].T; o += aw
    wv_t = wb[:, o:o + aw].T; o += aw
    wpre = wb[:, o:o + 3 * bw]; o += 3 * bw
    wz = wb[:, o:o + bw]; o += bw
    wba = jnp.pad(wb[:, o:o + 2 * B_HEADS], ((0, 0), (0, LANES - 2 * B_HEADS))); o += 2 * B_HEADS
    wgate = wb[:, o:o + 2 * d]
    row = lambda w: pl.BlockSpec((tile, w), lambda i: (i, 0))
    col = pl.BlockSpec((half, tile), lambda i: (0, i))
    tspec = pl.BlockSpec((1, 1, aw, tile), lambda i: (i // nt, i % nt, 0, 0))
    chunk = MXU_COLS if bw % MXU_COLS == 0 and d % MXU_COLS == 0 else LANES
    return pl.pallas_call(
        functools.partial(_inproj_kernel, tiles_per_seq=nt, chunk=chunk),
        out_shape=(jax.ShapeDtypeStruct((batch, nt, aw, tile), BF16),
                   jax.ShapeDtypeStruct((n, aw), BF16),
                   jax.ShapeDtypeStruct((batch, nt, aw, tile), BF16),
                   jax.ShapeDtypeStruct((n, 3 * bw), BF16),
                   jax.ShapeDtypeStruct((n, bw), BF16),
                   jax.ShapeDtypeStruct((n, LANES), F32),
                   jax.ShapeDtypeStruct((n, 2 * d), BF16)),
        grid=(n // tile,),
        in_specs=[row(d), _const_spec((1, d)), col, col,
                  _const_spec(wq_t.shape), _const_spec(wk_t.shape), _const_spec(wv_t.shape),
                  _const_spec(wpre.shape), _const_spec(wz.shape), _const_spec(wba.shape),
                  _const_spec(wgate.shape), _const_spec(conv_w.shape)],
        out_specs=(tspec, row(aw), tspec, row(3 * bw), row(bw), row(LANES), row(2 * d)),
        scratch_shapes=[pltpu.VMEM((HALO, 3 * bw), F32)],
        compiler_params=pltpu.CompilerParams(dimension_semantics=("arbitrary",), vmem_limit_bytes=VMEM_LIMIT),
        name="inproj",
    )(x2d, nw, cos_t, sin_t, wq_t, wk_t, wv_t, wpre, wz, wba, wgate, conv_w)


def _attn_kernel(lam_ref, qT_ref, k_ref, vT_ref, subln_ref, o_ref, qm_ref, m_ref, l_ref, acc_ref, s_ref, mt_ref, *,
                 tile, cb, hp, nt):
    dh = A_HEAD_DIM
    dv = A_V_DIM
    ncb = 2 * tile // cb
    zero = jnp.zeros((dh, tile), BF16)

    def load_q(i):
        for hh in range(hp):
            qT = qT_ref[0, i, hh * 2 * dh:(hh + 1) * 2 * dh, :]
            qm_ref[:, hh * 2 * tile:hh * 2 * tile + tile] = jnp.concatenate([qT[:dh], zero], axis=0)
            qm_ref[:, hh * 2 * tile + tile:(hh + 1) * 2 * tile] = jnp.concatenate([zero, qT[dh:]], axis=0)

    def reset():
        m_ref[...] = jnp.full(m_ref.shape, -jnp.inf, F32)
        l_ref[...] = jnp.zeros(l_ref.shape, F32)
        acc_ref[...] = jnp.zeros(acc_ref.shape, F32)

    def step(cur, nxt, nxt_diag, cur_diag):
        for c in range(hp * ncb):
            hh = c // ncb
            sl = slice(c * cb, (c + 1) * cb)
            qo = (c * cb) % tile
            rows_n = min(tile, qo + cb) if nxt_diag else tile
            rows_c = min(tile, qo + cb) if cur_diag else tile
            if nxt is not None:
                k_next = k_ref[0, pl.ds(pl.multiple_of(nxt * tile, tile), rows_n), hh * 2 * dh:(hh + 1) * 2 * dh]
                s_new = _dot(k_next, qm_ref[:, sl])
            if cur is not None:
                m_prev = m_ref[:, sl]
                m_new = jnp.maximum(m_prev, mt_ref[:, sl])
                alpha = jnp.exp2(m_prev - m_new)
                p = jnp.exp2(s_ref[:rows_c, sl] - m_new)
                l_ref[:, sl] = alpha * l_ref[:, sl] + jnp.sum(p, axis=0, keepdims=True)
                v_cur = vT_ref[0, cur, hh * dv:(hh + 1) * dv, :rows_c]
                acc_ref[:, sl] = alpha * acc_ref[:, sl] + _dot(v_cur, p.astype(BF16))
                m_ref[:, sl] = m_new
            if nxt is not None:
                if nxt_diag:
                    kpos = lax.broadcasted_iota(jnp.int32, (rows_n, cb), 0)
                    qpos = lax.broadcasted_iota(jnp.int32, (rows_n, cb), 1) + qo
                    s_new = jnp.where(kpos <= qpos, s_new, -jnp.inf)
                s_ref[:rows_n, sl] = s_new
                mt_ref[:, sl] = jnp.max(s_new, axis=0, keepdims=True)

    def finalize(i):
        lam = lam_ref[0, 0]
        inv_l = 1.0 / l_ref[...]
        rows = pl.ds(pl.multiple_of(i * tile, tile), tile)
        for hh in range(hp):
            c1 = slice(hh * 2 * tile, hh * 2 * tile + tile)
            c2 = slice(hh * 2 * tile + tile, (hh + 1) * 2 * tile)
            o = acc_ref[:, c1] * inv_l[:, c1] - lam * (acc_ref[:, c2] * inv_l[:, c2])
            o = o * lax.rsqrt(jnp.mean(o * o, axis=0, keepdims=True) + NORM_EPS) * subln_ref[...]
            o_ref[rows, hh * dv:(hh + 1) * dv] = o.T.astype(o_ref.dtype)

    load_q(0)
    reset()
    step(None, 0, True, False)

    def q_body(i, carry):
        def kv_body(j, c):
            step(j, j + 1, False, False)
            return c

        lax.fori_loop(0, i - 1, kv_body, 0)

        @pl.when(i > 0)
        def _():
            step(i - 1, i, True, False)

        @pl.when(i < nt - 1)
        def _():
            load_q(i + 1)
            step(i, 0, False, True)

        @pl.when(i == nt - 1)
        def _():
            step(i, None, False, True)

        finalize(i)
        reset()
        return carry

    lax.fori_loop(0, nt, q_body, 0)


def _attention(lam, qT, k2d, vT, subln_col, batch, seq, tile):
    nt = seq // tile
    dv = A_V_DIM
    n = batch * seq
    hp = 4
    assert A_HEADS % hp == 0
    kern = functools.partial(_attn_kernel, tile=tile, cb=min(tile, MXU_COLS), hp=hp, nt=nt)
    lanes = hp * 2 * tile
    return pl.pallas_call(
        kern,
        out_shape=jax.ShapeDtypeStruct((n, A_HEADS * dv), BF16),
        grid=(batch, A_HEADS // hp),
        in_specs=[pl.BlockSpec(memory_space=pltpu.SMEM),
                  pl.BlockSpec((1, nt, hp * 2 * A_HEAD_DIM, tile), lambda b, h: (b, 0, h, 0)),
                  pl.BlockSpec((1, seq, hp * 2 * A_HEAD_DIM), lambda b, h: (b, 0, h)),
                  pl.BlockSpec((1, nt, hp * dv, tile), lambda b, h: (b, 0, h, 0)),
                  pl.BlockSpec((dv, 1), lambda b, h: (0, 0))],
        out_specs=pl.BlockSpec((seq, hp * dv), lambda b, h: (b, h)),
        scratch_shapes=[pltpu.VMEM((2 * A_HEAD_DIM, lanes), BF16),
                        pltpu.VMEM((1, lanes), F32),
                        pltpu.VMEM((1, lanes), F32),
                        pltpu.VMEM((dv, lanes), F32),
                        pltpu.VMEM((tile, lanes), F32),
                        pltpu.VMEM((1, lanes), F32)],
        compiler_params=pltpu.CompilerParams(dimension_semantics=("arbitrary",) * 2, vmem_limit_bytes=VMEM_LIMIT),
        name="diff_attention",
    )(lam, qT, k2d.reshape(batch, seq, -1), vT, subln_col)


def _merge_kernel(x_ref, oa_ref, ob_ref, gate_ref, wa_ref, wb_ref, wo_ref, o_ref):
    d = x_ref.shape[1]
    ya = _dot(oa_ref[...], wa_ref[...])
    yb = _dot(ob_ref[...], wb_ref[...])
    g = gate_ref[...].astype(F32)
    merged = _sigmoid(g[:, :d]) * ya + _sigmoid(g[:, d:]) * yb
    o_ref[...] = x_ref[...] + _dot(merged.astype(BF16), wo_ref[...])


def _merge(x2d, oa, ob, gates, wa, wb, wo, tile):
    n, d = x2d.shape
    row = lambda w: pl.BlockSpec((tile, w), lambda i: (i, 0))
    return pl.pallas_call(
        _merge_kernel,
        out_shape=jax.ShapeDtypeStruct((n, d), F32),
        grid=(n // tile,),
        in_specs=[row(d), row(oa.shape[1]), row(ob.shape[1]), row(2 * d),
                  _const_spec(wa.shape), _const_spec(wb.shape), _const_spec(wo.shape)],
        out_specs=row(d),
        compiler_params=pltpu.CompilerParams(dimension_semantics=("arbitrary",), vmem_limit_bytes=VMEM_LIMIT),
        name="merge_outproj",
    )(x2d, oa, ob, gates, wa.astype(BF16), wb.astype(BF16), wo.astype(BF16))


def _ffn_kernel(x_ref, nw_ref, wup_ref, cw_ref, cb_ref, wdown_ref, nf_ref, o_ref, ucar_ref, act_ref, *, tiles_per_seq,
                fchunk, apply_final):
    i = pl.program_id(0)
    tile, d = x_ref.shape
    dff = wdown_ref.shape[0]

    @pl.when(i % tiles_per_seq == 0)
    def _():
        ucar_ref[...] = jnp.zeros(ucar_ref.shape, F32)

    x = x_ref[...]
    h = (x * lax.rsqrt(jnp.mean(x * x, axis=-1, keepdims=True) + NORM_EPS) * nw_ref[...]).astype(BF16)

    def up(c):
        return [_dot(h, wup_ref[:, base + c * fchunk:base + (c + 1) * fchunk]) for base in (0, dff)]

    u_next = up(0)
    for c in range(dff // fchunk):
        u_cur = u_next
        if c + 1 < dff // fchunk:
            u_next = up(c + 1)
        halves = []
        for u, base in zip(u_cur, (0, dff)):
            sl = slice(base + c * fchunk, base + (c + 1) * fchunk)
            uext = jnp.concatenate([ucar_ref[:, sl], u], axis=0)
            cw = cw_ref[:, sl]
            y = cb_ref[:, sl] + cw[FFN_CONV_WIDTH - 1:FFN_CONV_WIDTH] * u
            for s in range(1, FFN_CONV_WIDTH):
                y = y + cw[FFN_CONV_WIDTH - 1 - s:FFN_CONV_WIDTH - s] * pltpu.roll(uext, s, axis=0)[HALO:]
            ucar_ref[:, sl] = u[tile - HALO:, :]
            halves.append(y)
        act_ref[:, c * fchunk:(c + 1) * fchunk] = (_silu(halves[0]) * halves[1]).astype(BF16)
    x2 = x + _dot(act_ref[...], wdown_ref[...])
    if apply_final:
        x2 = x2 * lax.rsqrt(jnp.mean(x2 * x2, axis=-1, keepdims=True) + NORM_EPS) * nf_ref[...]
    o_ref[...] = x2


def _ffn(x2d, nw, wup, cw, cb, wdown, nf, seq, tile, fchunk, apply_final):
    n, d = x2d.shape
    dff = wdown.shape[0]
    row = pl.BlockSpec((tile, d), lambda i: (i, 0))
    kern = functools.partial(_ffn_kernel, tiles_per_seq=seq // tile, fchunk=fchunk, apply_final=apply_final)
    return pl.pallas_call(
        kern,
        out_shape=jax.ShapeDtypeStruct((n, d), F32),
        grid=(n // tile,),
        in_specs=[row, _const_spec((1, d)), _const_spec(wup.shape), _const_spec(cw.shape),
                  _const_spec((1, 2 * dff)), _const_spec(wdown.shape), _const_spec((1, d))],
        out_specs=row,
        scratch_shapes=[pltpu.VMEM((HALO, 2 * dff), F32), pltpu.VMEM((tile, dff), BF16)],
        compiler_params=pltpu.CompilerParams(dimension_semantics=("arbitrary",), vmem_limit_bytes=VMEM_LIMIT),
        name="convffn",
    )(x2d, nw, wup.astype(BF16), cw, cb, wdown.astype(BF16), nf)


def _gdn_kernel(qkv_ref, z_ref, ba_ref, baT_ref, alog_ref, dtb_ref, alogc_ref, dtbc_ref, onorm_ref, o_ref,
                state_ref, lhs_ref, kf_ref, dec_ref, x0_ref, qeg_ref, kdec_ref, egl_ref, *, nb):
    c = GDN_CHUNK
    dk = B_HEAD_DIM
    bw = B_HEADS * dk
    chains = range(nb * B_HEADS)
    prepared = (lhs_ref, kf_ref, dec_ref, x0_ref, qeg_ref, kdec_ref, egl_ref)

    @pl.when(pl.program_id(1) == 0)
    def _():
        state_ref[...] = jnp.zeros(state_ref.shape, F32)
        for ref in prepared:
            ref[...] = jnp.zeros(ref.shape, ref.dtype)

    r = lax.broadcasted_iota(jnp.int32, (c, c), 0)
    q_ = lax.broadcasted_iota(jnp.int32, (c, c), 1)
    lower = r >= q_
    strict = r > q_
    tri_lo = lower.astype(F32)
    tri_up = (r <= q_).astype(F32)
    softplus = lambda t: jnp.maximum(t, 0.0) + jnp.log1p(jnp.exp(-jnp.abs(t)))
    nt = (((1,), (1,)), ((), ()))
    tn = (((0,), (0,)), ((), ()))

    lhs = [lhs_ref[i] for i in chains]
    kfs = [kf_ref[i] for i in chains]
    decays = [dec_ref[i] for i in chains]
    xs = [x0_ref[i] for i in chains]
    qeg = [qeg_ref[i] for i in chains]
    kdec = [kdec_ref[i] for i in chains]
    egl = [egl_ref[i][0:1, :] for i in chains]

    for s_i in range(nb):
        qkv = qkv_ref[s_i].astype(F32)
        ba = ba_ref[s_i]
        baT = baT_ref[s_i, 0]
        beta = _sigmoid(ba)
        g = -jnp.exp(alog_ref[...]) * softplus(ba + dtb_ref[...])
        gT = -jnp.exp(alogc_ref[...]) * softplus(baT + dtbc_ref[...])
        gc = jnp.dot(tri_lo, g, precision=lax.Precision.HIGHEST, preferred_element_type=F32)
        gcT = jnp.dot(gT, tri_up, precision=lax.Precision.HIGHEST, preferred_element_type=F32)
        for h in range(B_HEADS):
            i = s_i * B_HEADS + h
            qh = qkv[:, h * dk:(h + 1) * dk]
            kh = qkv[:, bw + h * dk:bw + (h + 1) * dk]
            vh = qkv[:, 2 * bw + h * dk:2 * bw + (h + 1) * dk]
            qn = qh * (lax.rsqrt(jnp.sum(qh * qh, axis=-1, keepdims=True) + NORM_EPS) * (dk ** -0.5))
            kn = kh * lax.rsqrt(jnp.sum(kh * kh, axis=-1, keepdims=True) + NORM_EPS)
            gl = B_HEADS + h
            bh = beta[:, h:h + 1]
            gch = gc[:, gl:gl + 1]
            glast = gc[c - 1:c, gl:gl + 1]
            diff = gch - gcT[gl:gl + 1, :]
            eg = jnp.exp(gch)
            kb = kn * bh
            lhs_ref[i] = jnp.concatenate([kb, qn], axis=0).astype(BF16)
            kf_ref[i] = kn.astype(BF16)
            dec_ref[i] = jnp.where(lower, jnp.exp(jnp.where(lower, diff, 0.0)), 0.0)
            x0_ref[i] = jnp.concatenate([vh * bh, kb * eg], axis=1)
            qeg_ref[i] = (qn * eg).astype(BF16)
            kdec_ref[i] = (kn * jnp.exp(glast - gch)).astype(BF16)
            egl_ref[i] = jnp.broadcast_to(jnp.exp(glast), egl_ref.shape[1:])

    kq = [lax.dot_general(lhs[i], kfs[i], nt, preferred_element_type=F32) for i in chains]
    a_mats = [jnp.where(strict, kq[i][:c] * decays[i], 0.0) for i in chains]
    qks = [jnp.where(lower, kq[i][c:] * decays[i], 0.0).astype(BF16) for i in chains]
    same = lambda n: (r // n) == (q_ // n)
    eye = (r == q_).astype(F32)
    base = same(SOLVE_BASE)
    bds = [jnp.where(base, a, 0.0) for a in a_mats]
    pws = [b.astype(BF16) for b in bds]
    ts = [eye - b for b in bds]
    for _ in range(int(math.log2(SOLVE_BASE)) - 1):
        pw32 = [_dot(p, p) for p in pws]
        pws = [p.astype(BF16) for p in pw32]
        ts = [_dot(ts[i].astype(BF16), (eye + pw32[i]).astype(BF16)) for i in chains]
    size = SOLVE_BASE
    while size < c:
        sub = same(2 * size) & ((r // size) % 2 == 1) & ((q_ // size) % 2 == 0)
        tb = [t.astype(BF16) for t in ts]
        lt = [_dot(jnp.where(sub, a_mats[i], 0.0).astype(BF16), tb[i]).astype(BF16) for i in chains]
        ts = [ts[i] - _dot(tb[i], lt[i]) for i in chains]
        size *= 2
    xs = [_dot(ts[i].astype(BF16), xs[i].astype(BF16)) for i in chains]
    states = [state_ref[i] for i in chains]
    wq = [jnp.concatenate([xs[i][:, dk:].astype(BF16), qeg[i]], axis=0) for i in chains]
    ws = [_dot(wq[i], states[i].astype(BF16)) for i in chains]
    v_new = [(xs[i][:, :dk] - ws[i][:c]).astype(BF16) for i in chains]
    for i in chains:
        state_ref[i] = states[i] * egl[i] + lax.dot_general(kdec[i], v_new[i], tn, preferred_element_type=F32)
    outs = [ws[i][c:] + _dot(qks[i], v_new[i]) for i in chains]
    onorm = onorm_ref[...]
    for s_i in range(nb):
        z = z_ref[s_i].astype(F32)
        for h in range(B_HEADS):
            o = outs[s_i * B_HEADS + h]
            o = o * lax.rsqrt(jnp.mean(o * o, axis=-1, keepdims=True) + NORM_EPS) * onorm
            o_ref[s_i, :, h * dk:(h + 1) * dk] = (o * _silu(z[:, h * dk:(h + 1) * dk])).astype(o_ref.dtype)


def _gdn(qkv, z, ba, a_log, dt_bias, onorm, batch, seq):
    c = GDN_CHUNK
    dk = B_HEAD_DIM
    bw = B_HEADS * dk
    nb = 2 if batch % 2 == 0 else 1
    nchunk = seq // c
    nch = nb * B_HEADS
    baT = ba.reshape(batch, nchunk, c, LANES).transpose(0, 1, 3, 2)
    pad = lambda v: jnp.pad(v.astype(F32), (B_HEADS, LANES - 2 * B_HEADS))
    alog, dtb = pad(a_log), pad(dt_bias)
    prep = lambda w: pl.BlockSpec((nb, c, w), lambda b, t: (b, jnp.minimum(t, nchunk - 1), 0))
    done = lambda w: pl.BlockSpec((nb, c, w), lambda b, t: (b, jnp.maximum(t - 1, 0), 0))
    out = pl.pallas_call(
        functools.partial(_gdn_kernel, nb=nb),
        out_shape=jax.ShapeDtypeStruct((batch, seq, bw), BF16),
        grid=(batch // nb, nchunk + 1),
        in_specs=[prep(3 * bw), done(bw), prep(LANES),
                  pl.BlockSpec((nb, 1, LANES, c), lambda b, t: (b, jnp.minimum(t, nchunk - 1), 0, 0)),
                  _const_spec((1, LANES)), _const_spec((1, LANES)),
                  _const_spec((LANES, 1)), _const_spec((LANES, 1)), _const_spec((1, dk))],
        out_specs=done(bw),
        scratch_shapes=[pltpu.VMEM((nch, dk, dk), F32),
                        pltpu.VMEM((nch, 2 * c, dk), BF16),
                        pltpu.VMEM((nch, c, dk), BF16),
                        pltpu.VMEM((nch, c, c), F32),
                        pltpu.VMEM((nch, c, 2 * dk), F32),
                        pltpu.VMEM((nch, c, dk), BF16),
                        pltpu.VMEM((nch, c, dk), BF16),
                        pltpu.VMEM((nch, HALO, LANES), F32)],
        compiler_params=pltpu.CompilerParams(dimension_semantics=("arbitrary", "arbitrary"),
                                             vmem_limit_bytes=VMEM_LIMIT),
        name="gated_deltanet",
    )(qkv.reshape(batch, seq, -1), z.reshape(batch, seq, -1), ba.reshape(batch, seq, -1), baT,
      alog.reshape(1, -1), dtb.reshape(1, -1), alog.reshape(-1, 1), dtb.reshape(-1, 1), onorm.reshape(1, -1))
    return out.reshape(batch * seq, bw)


def _lam_kernel(q1_ref, k1_ref, q2_ref, k2_ref, o_ref, *, lambda_init):
    s1 = jnp.sum(q1_ref[...] * k1_ref[...], axis=-1, keepdims=True)
    s2 = jnp.sum(q2_ref[...] * k2_ref[...], axis=-1, keepdims=True)
    o_ref[...] = jnp.exp(s1) - jnp.exp(s2) + lambda_init


def _lambda(q1, k1, q2, k2, lambda_init):
    spec = pl.BlockSpec((1, q1.shape[-1]), lambda: (0, 0))
    return pl.pallas_call(
        functools.partial(_lam_kernel, lambda_init=lambda_init),
        out_shape=jax.ShapeDtypeStruct((1, 1), F32),
        in_specs=[spec] * 4,
        out_specs=pl.BlockSpec((1, 1), lambda: (0, 0)),
        name="lambda",
    )(q1.reshape(1, -1), k1.reshape(1, -1), q2.reshape(1, -1), k2.reshape(1, -1))


def _pick_tile(seq, pref):
    t = min(seq, pref)
    assert seq % t == 0
    return t


def kernel(x, positions, norm_mix, w_in, lambda_q1, lambda_k1, lambda_q2, lambda_k2, a_subln, w_a_out, conv_qkv,
           a_log, dt_bias, b_onorm, w_b_out, w_o, norm_ffn, w_up, ffn_conv, ffn_conv_bias, w_down, norm_final):
    batch, seq, d = x.shape
    depth = w_in.shape[0]
    tile = _pick_tile(seq, 512)
    x2d = x.reshape(batch * seq, d)
    tables = _rope_tables(positions, tile)
    for layer in range(depth):
        lambda_init = 0.8 - 0.6 * math.exp(-0.3 * layer)
        qT, k2d, vT, qkv, z, ba, gates = _inproj(x2d, norm_mix[layer].reshape(1, d), tables, w_in[layer],
                                                 conv_qkv[layer], batch, seq, tile)
        lam = _lambda(lambda_q1[layer], lambda_k1[layer], lambda_q2[layer], lambda_k2[layer], lambda_init)
        subln = (a_subln[layer] * (1.0 - lambda_init)).reshape(-1, 1)
        oa = _attention(lam, qT, k2d, vT, subln, batch, seq, tile)
        ob = _gdn(qkv, z, ba, a_log[layer], dt_bias[layer], b_onorm[layer], batch, seq)
        x2d = _merge(x2d, oa, ob, gates, w_a_out[layer], w_b_out[layer], w_o[layer], tile)
        dff = w_down.shape[1]
        x2d = _ffn(x2d, norm_ffn[layer].reshape(1, d), w_up[layer], ffn_conv[layer],
                   ffn_conv_bias[layer].reshape(1, -1), w_down[layer], norm_final.reshape(1, d), seq, tile,
                   fchunk=256 if dff % 256 == 0 else LANES, apply_final=layer == depth - 1)
    return x2d.reshape(batch, seq, d)
```

```python
import functools
import math

import jax
import jax.numpy as jnp
from jax import lax
from jax.experimental import pallas as pl
from jax.experimental.pallas import tpu as pltpu

NORM_EPS = 1e-6
ROPE_THETA = 10000.0
LANES = 128
MXU_COLS = 256
LOG2E = math.log2(math.e)
VMEM_LIMIT = 56 * 1024 * 1024

A_HEADS = 8
A_HEAD_DIM = 64
A_V_DIM = 128
B_HEADS = 8
B_HEAD_DIM = 128
CONV_WIDTH = 4
FFN_CONV_WIDTH = 3
GDN_CHUNK = 128
SOLVE_BASE = 8
HALO = 8
F32 = jnp.float32
BF16 = jnp.bfloat16


def _dot(a, b):
    return jnp.dot(a, b, preferred_element_type=F32)


def _const_spec(shape):
    nd = len(shape)
    return pl.BlockSpec(shape, lambda *_: (0,) * nd, pipeline_mode=pl.Buffered(1))


def _sigmoid(x):
    return 1.0 / (1.0 + jnp.exp(-x))


def _silu(x):
    return x * _sigmoid(x)


def _rope_kernel(pos_ref, freq_ref, cos_ref, sin_ref):
    ang = freq_ref[...] * pos_ref[...].astype(F32)
    cos_ref[...] = jnp.cos(ang)
    sin_ref[...] = jnp.sin(ang)


def _rope_tables(positions, tile):
    n = positions.size
    half = A_HEAD_DIM // 2
    inv_freq = ROPE_THETA ** (-jnp.arange(0, A_HEAD_DIM, 2, dtype=F32) / A_HEAD_DIM)
    col = pl.BlockSpec((half, tile), lambda i: (0, i))
    return pl.pallas_call(
        _rope_kernel,
        out_shape=(jax.ShapeDtypeStruct((half, n), F32),) * 2,
        grid=(n // tile,),
        in_specs=[pl.BlockSpec((1, tile), lambda i: (0, i)), pl.BlockSpec((half, 1), lambda i: (0, 0))],
        out_specs=(col, col),
        name="rope_tables",
    )(positions.reshape(1, n), inv_freq.reshape(half, 1))


def _rotary_t(y, cos_t, sin_t, scale):
    half = A_HEAD_DIM // 2
    outs = []
    for g in range(y.shape[0] // A_HEAD_DIM):
        x1 = y[g * A_HEAD_DIM:g * A_HEAD_DIM + half]
        x2 = y[g * A_HEAD_DIM + half:(g + 1) * A_HEAD_DIM]
        outs.append((x1 * cos_t - x2 * sin_t) * scale)
        outs.append((x2 * cos_t + x1 * sin_t) * scale)
    return jnp.concatenate(outs, axis=0)


def _dot_nt(a, b):
    return lax.dot_general(a, b, (((1,), (1,)), ((), ())), preferred_element_type=F32)


def _inproj_kernel(x_ref, nw_ref, cosT_ref, sinT_ref, wqT_ref, wkT_ref, wvT_ref, wpre_ref, wz_ref, wba_ref,
                   wgate_ref, cw_ref, qT_ref, k_ref, vT_ref, qkv_ref, z_ref, ba_ref, gate_ref, carry_ref, *,
                   tiles_per_seq, chunk):
    tile = x_ref.shape[0]

    @pl.when(pl.program_id(0) % tiles_per_seq == 0)
    def _():
        carry_ref[...] = jnp.zeros(carry_ref.shape, F32)

    x = x_ref[...]
    h = (x * lax.rsqrt(jnp.mean(x * x, axis=-1, keepdims=True) + NORM_EPS) * nw_ref[...]).astype(BF16)
    ba_ref[...] = _dot(h, wba_ref[...])

    cos_t, sin_t = cosT_ref[...], sinT_ref[...]

    def q_job(sl):
        def fin(u):
            qT_ref[0, 0, sl, :] = _rotary_t(u, cos_t, sin_t, A_HEAD_DIM ** -0.5 * LOG2E).astype(BF16)
        return (lambda: _dot_nt(wqT_ref[sl, :], h)), fin

    def v_job(sl):
        def fin(u):
            vT_ref[0, 0, sl, :] = u.astype(BF16)
        return (lambda: _dot_nt(wvT_ref[sl, :], h)), fin

    def k_job(sl):
        def fin(u):
            k_ref[:, sl] = _rotary_t(u, cos_t, sin_t, 1.0).T.astype(BF16)
        return (lambda: _dot_nt(wkT_ref[sl, :], h)), fin

    def plain_job(w_ref, o_ref, sl):
        def fin(u):
            o_ref[:, sl] = u.astype(o_ref.dtype)
        return (lambda: _dot(h, w_ref[:, sl])), fin

    def conv_job(sl):
        def fin(u):
            ext = jnp.concatenate([carry_ref[:, sl], u], axis=0)
            cw = cw_ref[:, sl]
            y = cw[CONV_WIDTH - 1:CONV_WIDTH] * u
            for s in range(1, CONV_WIDTH):
                y = y + cw[CONV_WIDTH - 1 - s:CONV_WIDTH - s] * pltpu.roll(ext, s, axis=0)[HALO:]
            carry_ref[:, sl] = u[tile - HALO:, :]
            qkv_ref[:, sl] = _silu(y).astype(qkv_ref.dtype)
        return (lambda: _dot(h, wpre_ref[:, sl])), fin

    chunks = lambda width: [slice(c * chunk, (c + 1) * chunk) for c in range(width // chunk)]
    light = ([q_job(sl) for sl in chunks(wqT_ref.shape[0])] + [k_job(sl) for sl in chunks(wkT_ref.shape[0])]
             + [v_job(sl) for sl in chunks(wvT_ref.shape[0])]
             + [plain_job(wz_ref, z_ref, sl) for sl in chunks(wz_ref.shape[1])]
             + [plain_job(wgate_ref, gate_ref, sl) for sl in chunks(wgate_ref.shape[1])])
    heavy = [conv_job(sl) for sl in chunks(wpre_ref.shape[1])]
    per = max(1, len(light) // len(heavy))
    jobs = []
    for n, hv in enumerate(heavy):
        jobs += light[n * per:(n + 1) * per] + [hv]
    jobs += light[len(heavy) * per:]
    u_next = jobs[0][0]()
    for n, (_, fin) in enumerate(jobs):
        u = u_next
        if n + 1 < len(jobs):
            u_next = jobs[n + 1][0]()
        fin(u)


def _inproj(x2d, nw, tables, w_in, conv_w, batch, seq, tile):
    n, d = x2d.shape
    nt = seq // tile
    aw = A_HEADS * 2 * A_HEAD_DIM
    bw = B_HEADS * B_HEAD_DIM
    half = A_HEAD_DIM // 2
    cos_t, sin_t = tables
    wb = w_in.astype(BF16)
    o = 0
    wq_t = wb[:, o:o + aw].T; o += aw
    wk_t = wb[:, o:o + aw].T; o += aw
    wv_t = wb[:, o:o + aw].T; o += aw
    wpre = wb[:, o:o + 3 * bw]; o += 3 * bw
    wz = wb[:, o:o + bw]; o += bw
    wba = jnp.pad(wb[:, o:o + 2 * B_HEADS], ((0, 0), (0, LANES - 2 * B_HEADS))); o += 2 * B_HEADS
    wgate = wb[:, o:o + 2 * d]
    row = lambda w: pl.BlockSpec((tile, w), lambda i: (i, 0))
    col = pl.BlockSpec((half, tile), lambda i: (0, i))
    tspec = pl.BlockSpec((1, 1, aw, tile), lambda i: (i // nt, i % nt, 0, 0))
    chunk = MXU_COLS if bw % MXU_COLS == 0 and d % MXU_COLS == 0 else LANES
    return pl.pallas_call(
        functools.partial(_inproj_kernel, tiles_per_seq=nt, chunk=chunk),
        out_shape=(jax.ShapeDtypeStruct((batch, nt, aw, tile), BF16),
                   jax.ShapeDtypeStruct((n, aw), BF16),
                   jax.ShapeDtypeStruct((batch, nt, aw, tile), BF16),
                   jax.ShapeDtypeStruct((n, 3 * bw), BF16),
                   jax.ShapeDtypeStruct((n, bw), BF16),
                   jax.ShapeDtypeStruct((n, LANES), F32),
                   jax.ShapeDtypeStruct((n, 2 * d), BF16)),
        grid=(n // tile,),
        in_specs=[row(d), _const_spec((1, d)), col, col,
                  _const_spec(wq_t.shape), _const_spec(wk_t.shape), _const_spec(wv_t.shape),
                  _const_spec(wpre.shape), _const_spec(wz.shape), _const_spec(wba.shape),
                  _const_spec(wgate.shape), _const_spec(conv_w.shape)],
        out_specs=(tspec, row(aw), tspec, row(3 * bw), row(bw), row(LANES), row(2 * d)),
        scratch_shapes=[pltpu.VMEM((HALO, 3 * bw), F32)],
        compiler_params=pltpu.CompilerParams(dimension_semantics=("arbitrary",), vmem_limit_bytes=VMEM_LIMIT),
        name="inproj",
    )(x2d, nw, cos_t, sin_t, wq_t, wk_t, wv_t, wpre, wz, wba, wgate, conv_w)


def _attn_kernel(lam_ref, qT_ref, k_ref, vT_ref, subln_ref, o_ref, qm_ref, m_ref, l_ref, acc_ref, s_ref, mt_ref, *,
                 tile, cb, hp, nt):
    dh = A_HEAD_DIM
    dv = A_V_DIM
    ncb = 2 * tile // cb
    zero = jnp.zeros((dh, tile), BF16)

    def load_q(i):
        for hh in range(hp):
            qT = qT_ref[0, i, hh * 2 * dh:(hh + 1) * 2 * dh, :]
            qm_ref[:, hh * 2 * tile:hh * 2 * tile + tile] = jnp.concatenate([qT[:dh], zero], axis=0)
            qm_ref[:, hh * 2 * tile + tile:(hh + 1) * 2 * tile] = jnp.concatenate([zero, qT[dh:]], axis=0)

    def reset():
        m_ref[...] = jnp.full(m_ref.shape, -jnp.inf, F32)
        l_ref[...] = jnp.zeros(l_ref.shape, F32)
        acc_ref[...] = jnp.zeros(acc_ref.shape, F32)

    def step(cur, nxt, nxt_diag, cur_diag):
        for c in range(hp * ncb):
            hh = c // ncb
            sl = slice(c * cb, (c + 1) * cb)
            qo = (c * cb) % tile
            rows_n = min(tile, qo + cb) if nxt_diag else tile
            rows_c = min(tile, qo + cb) if cur_diag else tile
            if nxt is not None:
                k_next = k_ref[0, pl.ds(pl.multiple_of(nxt * tile, tile), rows_n), hh * 2 * dh:(hh + 1) * 2 * dh]
                s_new = _dot(k_next, qm_ref[:, sl])
            if cur is not None:
                m_prev = m_ref[:, sl]
                m_new = jnp.maximum(m_prev, mt_ref[:, sl])
                alpha = jnp.exp2(m_prev - m_new)
                p = jnp.exp2(s_ref[:rows_c, sl] - m_new)
                l_ref[:, sl] = alpha * l_ref[:, sl] + jnp.sum(p, axis=0, keepdims=True)
                v_cur = vT_ref[0, cur, hh * dv:(hh + 1) * dv, :rows_c]
                acc_ref[:, sl] = alpha * acc_ref[:, sl] + _dot(v_cur, p.astype(BF16))
                m_ref[:, sl] = m_new
            if nxt is not None:
                if nxt_diag:
                    kpos = lax.broadcasted_iota(jnp.int32, (rows_n, cb), 0)
                    qpos = lax.broadcasted_iota(jnp.int32, (rows_n, cb), 1) + qo
                    s_new = jnp.where(kpos <= qpos, s_new, -jnp.inf)
                s_ref[:rows_n, sl] = s_new
                mt_ref[:, sl] = jnp.max(s_new, axis=0, keepdims=True)

    def finalize(i):
        lam = lam_ref[0, 0]
        inv_l = 1.0 / l_ref[...]
        rows = pl.ds(pl.multiple_of(i * tile, tile), tile)
        for hh in range(hp):
            c1 = slice(hh * 2 * tile, hh * 2 * tile + tile)
            c2 = slice(hh * 2 * tile + tile, (hh + 1) * 2 * tile)
            o = acc_ref[:, c1] * inv_l[:, c1] - lam * (acc_ref[:, c2] * inv_l[:, c2])
            o = o * lax.rsqrt(jnp.mean(o * o, axis=0, keepdims=True) + NORM_EPS) * subln_ref[...]
            o_ref[rows, hh * dv:(hh + 1) * dv] = o.T.astype(o_ref.dtype)

    load_q(0)
    reset()
    step(None, 0, True, False)

    def q_body(i, carry):
        def kv_body(j, c):
            step(j, j + 1, False, False)
            return c

        lax.fori_loop(0, i - 1, kv_body, 0)

        @pl.when(i > 0)
        def _():
            step(i - 1, i, True, False)

        @pl.when(i < nt - 1)
        def _():
            load_q(i + 1)
            step(i, 0, False, True)

        @pl.when(i == nt - 1)
        def _():
            step(i, None, False, True)

        finalize(i)
        reset()
        return carry

    lax.fori_loop(0, nt, q_body, 0)


def _attention(lam, qT, k2d, vT, subln_col, batch, seq, tile):
    nt = seq // tile
    dv = A_V_DIM
    n = batch * seq
    hp = 4
    assert A_HEADS % hp == 0
    kern = functools.partial(_attn_kernel, tile=tile, cb=min(tile, MXU_COLS), hp=hp, nt=nt)
    lanes = hp * 2 * tile
    return pl.pallas_call(
        kern,
        out_shape=jax.ShapeDtypeStruct((n, A_HEADS * dv), BF16),
        grid=(batch, A_HEADS // hp),
        in_specs=[pl.BlockSpec(memory_space=pltpu.SMEM),
                  pl.BlockSpec((1, nt, hp * 2 * A_HEAD_DIM, tile), lambda b, h: (b, 0, h, 0)),
                  pl.BlockSpec((1, seq, hp * 2 * A_HEAD_DIM), lambda b, h: (b, 0, h)),
                  pl.BlockSpec((1, nt, hp * dv, tile), lambda b, h: (b, 0, h, 0)),
                  pl.BlockSpec((dv, 1), lambda b, h: (0, 0))],
        out_specs=pl.BlockSpec((seq, hp * dv), lambda b, h: (b, h)),
        scratch_shapes=[pltpu.VMEM((2 * A_HEAD_DIM, lanes), BF16),
                        pltpu.VMEM((1, lanes), F32),
                        pltpu.VMEM((1, lanes), F32),
                        pltpu.VMEM((dv, lanes + LANES), F32),
                        pltpu.VMEM((tile, lanes + LANES), F32),
                        pltpu.VMEM((1, lanes), F32)],
        compiler_params=pltpu.CompilerParams(dimension_semantics=("arbitrary",) * 2, vmem_limit_bytes=VMEM_LIMIT),
        name="diff_attention",
    )(lam, qT, k2d.reshape(batch, seq, -1), vT, subln_col)


def _merge_kernel(x_ref, oa_ref, ob_ref, gate_ref, wa_ref, wb_ref, wo_ref, o_ref):
    d = x_ref.shape[1]
    ya = _dot(oa_ref[...], wa_ref[...])
    yb = _dot(ob_ref[...], wb_ref[...])
    g = gate_ref[...].astype(F32)
    merged = _sigmoid(g[:, :d]) * ya + _sigmoid(g[:, d:]) * yb
    o_ref[...] = x_ref[...] + _dot(merged.astype(BF16), wo_ref[...])


def _merge(x2d, oa, ob, gates, wa, wb, wo, tile):
    n, d = x2d.shape
    row = lambda w: pl.BlockSpec((tile, w), lambda i: (i, 0))
    return pl.pallas_call(
        _merge_kernel,
        out_shape=jax.ShapeDtypeStruct((n, d), F32),
        grid=(n // tile,),
        in_specs=[row(d), row(oa.shape[1]), row(ob.shape[1]), row(2 * d),
                  _const_spec(wa.shape), _const_spec(wb.shape), _const_spec(wo.shape)],
        out_specs=row(d),
        compiler_params=pltpu.CompilerParams(dimension_semantics=("arbitrary",), vmem_limit_bytes=VMEM_LIMIT),
        name="merge_outproj",
    )(x2d, oa, ob, gates, wa.astype(BF16), wb.astype(BF16), wo.astype(BF16))


def _ffn_kernel(x_ref, nw_ref, wup_ref, cw_ref, cb_ref, wdown_ref, nf_ref, o_ref, ucar_ref, act_ref, *, tiles_per_seq,
                fchunk, apply_final):
    i = pl.program_id(0)
    tile, d = x_ref.shape
    dff = wdown_ref.shape[0]

    @pl.when(i % tiles_per_seq == 0)
    def _():
        ucar_ref[...] = jnp.zeros(ucar_ref.shape, F32)

    x = x_ref[...]
    h = (x * lax.rsqrt(jnp.mean(x * x, axis=-1, keepdims=True) + NORM_EPS) * nw_ref[...]).astype(BF16)

    def up(c):
        return [_dot(h, wup_ref[:, base + c * fchunk:base + (c + 1) * fchunk]) for base in (0, dff)]

    u_next = up(0)
    for c in range(dff // fchunk):
        u_cur = u_next
        if c + 1 < dff // fchunk:
            u_next = up(c + 1)
        halves = []
        for u, base in zip(u_cur, (0, dff)):
            sl = slice(base + c * fchunk, base + (c + 1) * fchunk)
            uext = jnp.concatenate([ucar_ref[:, sl], u], axis=0)
            cw = cw_ref[:, sl]
            y = cb_ref[:, sl] + cw[FFN_CONV_WIDTH - 1:FFN_CONV_WIDTH] * u
            for s in range(1, FFN_CONV_WIDTH):
                y = y + cw[FFN_CONV_WIDTH - 1 - s:FFN_CONV_WIDTH - s] * pltpu.roll(uext, s, axis=0)[HALO:]
            ucar_ref[:, sl] = u[tile - HALO:, :]
            halves.append(y)
        act_ref[:, c * fchunk:(c + 1) * fchunk] = (_silu(halves[0]) * halves[1]).astype(BF16)
    x2 = x + _dot(act_ref[...], wdown_ref[...])
    if apply_final:
        x2 = x2 * lax.rsqrt(jnp.mean(x2 * x2, axis=-1, keepdims=True) + NORM_EPS) * nf_ref[...]
    o_ref[...] = x2


def _ffn(x2d, nw, wup, cw, cb, wdown, nf, seq, tile, fchunk, apply_final):
    n, d = x2d.shape
    dff = wdown.shape[0]
    row = pl.BlockSpec((tile, d), lambda i: (i, 0))
    kern = functools.partial(_ffn_kernel, tiles_per_seq=seq // tile, fchunk=fchunk, apply_final=apply_final)
    return pl.pallas_call(
        kern,
        out_shape=jax.ShapeDtypeStruct((n, d), F32),
        grid=(n // tile,),
        in_specs=[row, _const_spec((1, d)), _const_spec(wup.shape), _const_spec(cw.shape),
                  _const_spec((1, 2 * dff)), _const_spec(wdown.shape), _const_spec((1, d))],
        out_specs=row,
        scratch_shapes=[pltpu.VMEM((HALO, 2 * dff), F32), pltpu.VMEM((tile, dff), BF16)],
        compiler_params=pltpu.CompilerParams(dimension_semantics=("arbitrary",), vmem_limit_bytes=VMEM_LIMIT),
        name="convffn",
    )(x2d, nw, wup.astype(BF16), cw, cb, wdown.astype(BF16), nf)


def _gdn_kernel(qkv_ref, z_ref, ba_ref, baT_ref, alog_ref, dtb_ref, alogc_ref, dtbc_ref, onorm_ref, o_ref,
                state_ref, lhs_ref, kf_ref, dec_ref, x0_ref, qeg_ref, kdec_ref, egl_ref, *, nb):
    c = GDN_CHUNK
    dk = B_HEAD_DIM
    bw = B_HEADS * dk
    chains = range(nb * B_HEADS)
    prepared = (lhs_ref, kf_ref, dec_ref, x0_ref, qeg_ref, kdec_ref, egl_ref)

    @pl.when(pl.program_id(1) == 0)
    def _():
        state_ref[...] = jnp.zeros(state_ref.shape, F32)
        for ref in prepared:
            ref[...] = jnp.zeros(ref.shape, ref.dtype)

    r = lax.broadcasted_iota(jnp.int32, (c, c), 0)
    q_ = lax.broadcasted_iota(jnp.int32, (c, c), 1)
    lower = r >= q_
    strict = r > q_
    tri_lo = lower.astype(F32)
    tri_up = (r <= q_).astype(F32)
    softplus = lambda t: jnp.maximum(t, 0.0) + jnp.log1p(jnp.exp(-jnp.abs(t)))
    nt = (((1,), (1,)), ((), ()))
    tn = (((0,), (0,)), ((), ()))

    lhs = [lhs_ref[i] for i in chains]
    kfs = [kf_ref[i] for i in chains]
    decays = [dec_ref[i] for i in chains]
    xs = [x0_ref[i] for i in chains]
    qeg = [qeg_ref[i] for i in chains]
    kdec = [kdec_ref[i] for i in chains]
    egl = [egl_ref[i][0:1, :] for i in chains]

    for s_i in range(nb):
        qkv = qkv_ref[s_i].astype(F32)
        ba = ba_ref[s_i]
        baT = baT_ref[s_i, 0]
        beta = _sigmoid(ba)
        g = -jnp.exp(alog_ref[...]) * softplus(ba + dtb_ref[...])
        gT = -jnp.exp(alogc_ref[...]) * softplus(baT + dtbc_ref[...])
        gc = jnp.dot(tri_lo, g, precision=lax.Precision.HIGHEST, preferred_element_type=F32)
        gcT = jnp.dot(gT, tri_up, precision=lax.Precision.HIGHEST, preferred_element_type=F32)
        for h in range(B_HEADS):
            i = s_i * B_HEADS + h
            qh = qkv[:, h * dk:(h + 1) * dk]
            kh = qkv[:, bw + h * dk:bw + (h + 1) * dk]
            vh = qkv[:, 2 * bw + h * dk:2 * bw + (h + 1) * dk]
            qn = qh * (lax.rsqrt(jnp.sum(qh * qh, axis=-1, keepdims=True) + NORM_EPS) * (dk ** -0.5))
            kn = kh * lax.rsqrt(jnp.sum(kh * kh, axis=-1, keepdims=True) + NORM_EPS)
            gl = B_HEADS + h
            bh = beta[:, h:h + 1]
            gch = gc[:, gl:gl + 1]
            glast = gc[c - 1:c, gl:gl + 1]
            diff = gch - gcT[gl:gl + 1, :]
            eg = jnp.exp(gch)
            kb = kn * bh
            lhs_ref[i] = jnp.concatenate([kb, qn], axis=0).astype(BF16)
            kf_ref[i] = kn.astype(BF16)
            dec_ref[i] = jnp.where(lower, jnp.exp(jnp.where(lower, diff, 0.0)), 0.0)
            x0_ref[i] = jnp.concatenate([vh * bh, kb * eg], axis=1)
            qeg_ref[i] = (qn * eg).astype(BF16)
            kdec_ref[i] = (kn * jnp.exp(glast - gch)).astype(BF16)
            egl_ref[i] = jnp.broadcast_to(jnp.exp(glast), egl_ref.shape[1:])

    kq = [lax.dot_general(lhs[i], kfs[i], nt, preferred_element_type=F32) for i in chains]
    a_mats = [jnp.where(strict, kq[i][:c] * decays[i], 0.0) for i in chains]
    qks = [jnp.where(lower, kq[i][c:] * decays[i], 0.0).astype(BF16) for i in chains]
    same = lambda n: (r // n) == (q_ // n)
    eye = (r == q_).astype(F32)
    base = same(SOLVE_BASE)
    bds = [jnp.where(base, a, 0.0) for a in a_mats]
    pws = [b.astype(BF16) for b in bds]
    ts = [eye - b for b in bds]
    for _ in range(int(math.log2(SOLVE_BASE)) - 1):
        pw32 = [_dot(p, p) for p in pws]
        pws = [p.astype(BF16) for p in pw32]
        ts = [_dot(ts[i].astype(BF16), (eye + pw32[i]).astype(BF16)) for i in chains]
    size = SOLVE_BASE
    while size < c:
        sub = same(2 * size) & ((r // size) % 2 == 1) & ((q_ // size) % 2 == 0)
        tb = [t.astype(BF16) for t in ts]
        lt = [_dot(jnp.where(sub, a_mats[i], 0.0).astype(BF16), tb[i]).astype(BF16) for i in chains]
        ts = [ts[i] - _dot(tb[i], lt[i]) for i in chains]
        size *= 2
    xs = [_dot(ts[i].astype(BF16), xs[i].astype(BF16)) for i in chains]
    states = [state_ref[i] for i in chains]
    wq = [jnp.concatenate([xs[i][:, dk:].astype(BF16), qeg[i]], axis=0) for i in chains]
    ws = [_dot(wq[i], states[i].astype(BF16)) for i in chains]
    v_new = [(xs[i][:, :dk] - ws[i][:c]).astype(BF16) for i in chains]
    for i in chains:
        state_ref[i] = states[i] * egl[i] + lax.dot_general(kdec[i], v_new[i], tn, preferred_element_type=F32)
    outs = [ws[i][c:] + _dot(qks[i], v_new[i]) for i in chains]
    onorm = onorm_ref[...]
    for s_i in range(nb):
        z = z_ref[s_i].astype(F32)
        for h in range(B_HEADS):
            o = outs[s_i * B_HEADS + h]
            o = o * lax.rsqrt(jnp.mean(o * o, axis=-1, keepdims=True) + NORM_EPS) * onorm
            o_ref[s_i, :, h * dk:(h + 1) * dk] = (o * _silu(z[:, h * dk:(h + 1) * dk])).astype(o_ref.dtype)


def _gdn(qkv, z, ba, a_log, dt_bias, onorm, batch, seq):
    c = GDN_CHUNK
    dk = B_HEAD_DIM
    bw = B_HEADS * dk
    nb = 2 if batch % 2 == 0 else 1
    nchunk = seq // c
    nch = nb * B_HEADS
    baT = ba.reshape(batch, nchunk, c, LANES).transpose(0, 1, 3, 2)
    pad = lambda v: jnp.pad(v.astype(F32), (B_HEADS, LANES - 2 * B_HEADS))
    alog, dtb = pad(a_log), pad(dt_bias)
    prep = lambda w: pl.BlockSpec((nb, c, w), lambda b, t: (b, jnp.minimum(t, nchunk - 1), 0))
    done = lambda w: pl.BlockSpec((nb, c, w), lambda b, t: (b, jnp.maximum(t - 1, 0), 0))
    out = pl.pallas_call(
        functools.partial(_gdn_kernel, nb=nb),
        out_shape=jax.ShapeDtypeStruct((batch, seq, bw), BF16),
        grid=(batch // nb, nchunk + 1),
        in_specs=[prep(3 * bw), done(bw), prep(LANES),
                  pl.BlockSpec((nb, 1, LANES, c), lambda b, t: (b, jnp.minimum(t, nchunk - 1), 0, 0)),
                  _const_spec((1, LANES)), _const_spec((1, LANES)),
                  _const_spec((LANES, 1)), _const_spec((LANES, 1)), _const_spec((1, dk))],
        out_specs=done(bw),
        scratch_shapes=[pltpu.VMEM((nch, dk, dk), F32),
                        pltpu.VMEM((nch, 2 * c, dk), BF16),
                        pltpu.VMEM((nch, c, dk), BF16),
                        pltpu.VMEM((nch, c, c), F32),
                        pltpu.VMEM((nch, c, 2 * dk), F32),
                        pltpu.VMEM((nch, c, dk), BF16),
                        pltpu.VMEM((nch, c, dk), BF16),
                        pltpu.VMEM((nch, HALO, LANES), F32)],
        compiler_params=pltpu.CompilerParams(dimension_semantics=("arbitrary", "arbitrary"),
                                             vmem_limit_bytes=VMEM_LIMIT),
        name="gated_deltanet",
    )(qkv.reshape(batch, seq, -1), z.reshape(batch, seq, -1), ba.reshape(batch, seq, -1), baT,
      alog.reshape(1, -1), dtb.reshape(1, -1), alog.reshape(-1, 1), dtb.reshape(-1, 1), onorm.reshape(1, -1))
    return out.reshape(batch * seq, bw)


def _lam_kernel(q1_ref, k1_ref, q2_ref, k2_ref, o_ref, *, lambda_init):
    s1 = jnp.sum(q1_ref[...] * k1_ref[...], axis=-1, keepdims=True)
    s2 = jnp.sum(q2_ref[...] * k2_ref[...], axis=-1, keepdims=True)
    o_ref[...] = jnp.exp(s1) - jnp.exp(s2) + lambda_init


def _lambda(q1, k1, q2, k2, lambda_init):
    spec = pl.BlockSpec((1, q1.shape[-1]), lambda: (0, 0))
    return pl.pallas_call(
        functools.partial(_lam_kernel, lambda_init=lambda_init),
        out_shape=jax.ShapeDtypeStruct((1, 1), F32),
        in_specs=[spec] * 4,
        out_specs=pl.BlockSpec((1, 1), lambda: (0, 0)),
        name="lambda",
    )(q1.reshape(1, -1), k1.reshape(1, -1), q2.reshape(1, -1), k2.reshape(1, -1))


def _pick_tile(seq, pref):
    t = min(seq, pref)
    assert seq % t == 0
    return t


def kernel(x, positions, norm_mix, w_in, lambda_q1, lambda_k1, lambda_q2, lambda_k2, a_subln, w_a_out, conv_qkv,
           a_log, dt_bias, b_onorm, w_b_out, w_o, norm_ffn, w_up, ffn_conv, ffn_conv_bias, w_down, norm_final):
    batch, seq, d = x.shape
    depth = w_in.shape[0]
    tile = _pick_tile(seq, 512)
    x2d = x.reshape(batch * seq, d)
    tables = _rope_tables(positions, tile)
    for layer in range(depth):
        lambda_init = 0.8 - 0.6 * math.exp(-0.3 * layer)
        qT, k2d, vT, qkv, z, ba, gates = _inproj(x2d, norm_mix[layer].reshape(1, d), tables, w_in[layer],
                                                 conv_qkv[layer], batch, seq, tile)
        lam = _lambda(lambda_q1[layer], lambda_k1[layer], lambda_q2[layer], lambda_k2[layer], lambda_init)
        subln = (a_subln[layer] * (1.0 - lambda_init)).reshape(-1, 1)
        oa = _attention(lam, qT, k2d, vT, subln, batch, seq, tile)
        ob = _gdn(qkv, z, ba, a_log[layer], dt_bias[layer], b_onorm[layer], batch, seq)
        x2d = _merge(x2d, oa, ob, gates, w_a_out[layer], w_b_out[layer], w_o[layer], tile)
        dff = w_down.shape[1]
        x2d = _ffn(x2d, norm_ffn[layer].reshape(1, d), w_up[layer], ffn_conv[layer],
                   ffn_conv_bias[layer].reshape(1, -1), w_down[layer], norm_final.reshape(1, d), seq, tile,
                   fchunk=256 if dff % 256 == 0 else LANES, apply_final=layer == depth - 1)
    return x2d.reshape(batch, seq, d)
```

```python
import functools
import math

import jax
import jax.numpy as jnp
from jax import lax
from jax.experimental import pallas as pl
from jax.experimental.pallas import tpu as pltpu

NORM_EPS = 1e-6
ROPE_THETA = 10000.0
LANES = 128
MXU_COLS = 256
LOG2E = math.log2(math.e)
VMEM_LIMIT = 56 * 1024 * 1024

A_HEADS = 8
A_HEAD_DIM = 64
A_V_DIM = 128
B_HEADS = 8
B_HEAD_DIM = 128
CONV_WIDTH = 4
FFN_CONV_WIDTH = 3
GDN_CHUNK = 128
SOLVE_BASE = 8
HALO = 8
F32 = jnp.float32
BF16 = jnp.bfloat16


def _dot(a, b):
    return jnp.dot(a, b, preferred_element_type=F32)


def _const_spec(shape):
    nd = len(shape)
    return pl.BlockSpec(shape, lambda *_: (0,) * nd, pipeline_mode=pl.Buffered(1))


def _sigmoid(x):
    return 1.0 / (1.0 + jnp.exp(-x))


def _silu(x):
    return x * _sigmoid(x)


def _rope_kernel(pos_ref, freq_ref, cos_ref, sin_ref):
    ang = freq_ref[...] * pos_ref[...].astype(F32)
    cos_ref[...] = jnp.cos(ang)
    sin_ref[...] = jnp.sin(ang)


def _rope_tables(positions, tile):
    n = positions.size
    half = A_HEAD_DIM // 2
    inv_freq = ROPE_THETA ** (-jnp.arange(0, A_HEAD_DIM, 2, dtype=F32) / A_HEAD_DIM)
    col = pl.BlockSpec((half, tile), lambda i: (0, i))
    return pl.pallas_call(
        _rope_kernel,
        out_shape=(jax.ShapeDtypeStruct((half, n), F32),) * 2,
        grid=(n // tile,),
        in_specs=[pl.BlockSpec((1, tile), lambda i: (0, i)), pl.BlockSpec((half, 1), lambda i: (0, 0))],
        out_specs=(col, col),
        name="rope_tables",
    )(positions.reshape(1, n), inv_freq.reshape(half, 1))


def _rotary_t(y, cos_t, sin_t, scale):
    half = A_HEAD_DIM // 2
    outs = []
    for g in range(y.shape[0] // A_HEAD_DIM):
        x1 = y[g * A_HEAD_DIM:g * A_HEAD_DIM + half]
        x2 = y[g * A_HEAD_DIM + half:(g + 1) * A_HEAD_DIM]
        outs.append((x1 * cos_t - x2 * sin_t) * scale)
        outs.append((x2 * cos_t + x1 * sin_t) * scale)
    return jnp.concatenate(outs, axis=0)


def _dot_nt(a, b):
    return lax.dot_general(a, b, (((1,), (1,)), ((), ())), preferred_element_type=F32)


def _inproj_kernel(x_ref, nw_ref, cosT_ref, sinT_ref, wqT_ref, wkT_ref, wvT_ref, wpre_ref, wz_ref, wba_ref,
                   wgate_ref, cw_ref, qT_ref, k_ref, vT_ref, qkv_ref, z_ref, ba_ref, gate_ref, carry_ref, *,
                   tiles_per_seq, chunk):
    tile = x_ref.shape[0]

    @pl.when(pl.program_id(0) % tiles_per_seq == 0)
    def _():
        carry_ref[...] = jnp.zeros(carry_ref.shape, F32)

    x = x_ref[...]
    h = (x * lax.rsqrt(jnp.mean(x * x, axis=-1, keepdims=True) + NORM_EPS) * nw_ref[...]).astype(BF16)
    ba_ref[...] = _dot(h, wba_ref[...])

    cos_t, sin_t = cosT_ref[...], sinT_ref[...]

    def q_job(sl):
        def fin(u):
            qT_ref[0, 0, sl, :] = _rotary_t(u, cos_t, sin_t, A_HEAD_DIM ** -0.5 * LOG2E).astype(BF16)
        return (lambda: _dot_nt(wqT_ref[sl, :], h)), fin

    def v_job(sl):
        def fin(u):
            vT_ref[0, 0, sl, :] = u.astype(BF16)
        return (lambda: _dot_nt(wvT_ref[sl, :], h)), fin

    def k_job(sl):
        def fin(u):
            k_ref[:, sl] = _rotary_t(u, cos_t, sin_t, 1.0).T.astype(BF16)
        return (lambda: _dot_nt(wkT_ref[sl, :], h)), fin

    def plain_job(w_ref, o_ref, sl):
        def fin(u):
            o_ref[:, sl] = u.astype(o_ref.dtype)
        return (lambda: _dot(h, w_ref[:, sl])), fin

    def conv_job(sl):
        def fin(u):
            ext = jnp.concatenate([carry_ref[:, sl], u], axis=0)
            cw = cw_ref[:, sl]
            y = cw[CONV_WIDTH - 1:CONV_WIDTH] * u
            for s in range(1, CONV_WIDTH):
                y = y + cw[CONV_WIDTH - 1 - s:CONV_WIDTH - s] * pltpu.roll(ext, s, axis=0)[HALO:]
            carry_ref[:, sl] = u[tile - HALO:, :]
            qkv_ref[:, sl] = _silu(y).astype(qkv_ref.dtype)
        return (lambda: _dot(h, wpre_ref[:, sl])), fin

    chunks = lambda width: [slice(c * chunk, (c + 1) * chunk) for c in range(width // chunk)]
    light = ([q_job(sl) for sl in chunks(wqT_ref.shape[0])] + [k_job(sl) for sl in chunks(wkT_ref.shape[0])]
             + [v_job(sl) for sl in chunks(wvT_ref.shape[0])]
             + [plain_job(wz_ref, z_ref, sl) for sl in chunks(wz_ref.shape[1])]
             + [plain_job(wgate_ref, gate_ref, sl) for sl in chunks(wgate_ref.shape[1])])
    heavy = [conv_job(sl) for sl in chunks(wpre_ref.shape[1])]
    per = max(1, len(light) // len(heavy))
    jobs = []
    for n, hv in enumerate(heavy):
        jobs += light[n * per:(n + 1) * per] + [hv]
    jobs += light[len(heavy) * per:]
    u_next = jobs[0][0]()
    for n, (_, fin) in enumerate(jobs):
        u = u_next
        if n + 1 < len(jobs):
            u_next = jobs[n + 1][0]()
        fin(u)


def _inproj(x2d, nw, tables, w_in, conv_w, batch, seq, tile):
    n, d = x2d.shape
    nt = seq // tile
    aw = A_HEADS * 2 * A_HEAD_DIM
    bw = B_HEADS * B_HEAD_DIM
    half = A_HEAD_DIM // 2
    cos_t, sin_t = tables
    wb = w_in.astype(BF16)
    o = 0
    wq_t = wb[:, o:o + aw].T; o += aw
    wk_t = wb[:, o:o + aw].T; o += aw
    wv_t = wb[:, o:o + aw].T; o += aw
    wpre = wb[:, o:o + 3 * bw]; o += 3 * bw
    wz = wb[:, o:o + bw]; o += bw
    wba = jnp.pad(wb[:, o:o + 2 * B_HEADS], ((0, 0), (0, LANES - 2 * B_HEADS))); o += 2 * B_HEADS
    wgate = wb[:, o:o + 2 * d]
    row = lambda w: pl.BlockSpec((tile, w), lambda i: (i, 0))
    col = pl.BlockSpec((half, tile), lambda i: (0, i))
    tspec = pl.BlockSpec((1, 1, aw, tile), lambda i: (i // nt, i % nt, 0, 0))
    chunk = MXU_COLS if bw % MXU_COLS == 0 and d % MXU_COLS == 0 else LANES
    return pl.pallas_call(
        functools.partial(_inproj_kernel, tiles_per_seq=nt, chunk=chunk),
        out_shape=(jax.ShapeDtypeStruct((batch, nt, aw, tile), BF16),
                   jax.ShapeDtypeStruct((n, aw), BF16),
                   jax.ShapeDtypeStruct((batch, nt, aw, tile), BF16),
                   jax.ShapeDtypeStruct((n, 3 * bw), BF16),
                   jax.ShapeDtypeStruct((n, bw), BF16),
                   jax.ShapeDtypeStruct((n, LANES), F32),
                   jax.ShapeDtypeStruct((n, 2 * d), BF16)),
        grid=(n // tile,),
        in_specs=[row(d), _const_spec((1, d)), col, col,
                  _const_spec(wq_t.shape), _const_spec(wk_t.shape), _const_spec(wv_t.shape),
                  _const_spec(wpre.shape), _const_spec(wz.shape), _const_spec(wba.shape),
                  _const_spec(wgate.shape), _const_spec(conv_w.shape)],
        out_specs=(tspec, row(aw), tspec, row(3 * bw), row(bw), row(LANES), row(2 * d)),
        scratch_shapes=[pltpu.VMEM((HALO, 3 * bw), F32)],
        compiler_params=pltpu.CompilerParams(dimension_semantics=("arbitrary",), vmem_limit_bytes=VMEM_LIMIT),
        name="inproj",
    )(x2d, nw, cos_t, sin_t, wq_t, wk_t, wv_t, wpre, wz, wba, wgate, conv_w)


def _attn_kernel(lam_ref, qT_ref, k_ref, vT_ref, subln_ref, o_ref, qm_ref, m_ref, l_ref, acc_ref, s_ref, mt_ref, *,
                 tile, cb, hp, nt):
    dh = A_HEAD_DIM
    dv = A_V_DIM
    ncb = 2 * tile // cb
    zero = jnp.zeros((dh, tile), BF16)

    def load_q(i):
        for hh in range(hp):
            qT = qT_ref[0, i, hh * 2 * dh:(hh + 1) * 2 * dh, :]
            qm_ref[:, hh * 2 * tile:hh * 2 * tile + tile] = jnp.concatenate([qT[:dh], zero], axis=0)
            qm_ref[:, hh * 2 * tile + tile:(hh + 1) * 2 * tile] = jnp.concatenate([zero, qT[dh:]], axis=0)

    def reset():
        m_ref[...] = jnp.full(m_ref.shape, -jnp.inf, F32)
        l_ref[...] = jnp.zeros(l_ref.shape, F32)
        acc_ref[...] = jnp.zeros(acc_ref.shape, F32)

    def step(cur, nxt, nxt_diag, cur_diag):
        for c in range(hp * ncb):
            hh = c // ncb
            sl = slice(c * cb, (c + 1) * cb)
            qo = (c * cb) % tile
            rows_n = min(tile, qo + cb) if nxt_diag else tile
            rows_c = min(tile, qo + cb) if cur_diag else tile
            if nxt is not None:
                k_next = k_ref[0, pl.ds(pl.multiple_of(nxt * tile, tile), rows_n), hh * 2 * dh:(hh + 1) * 2 * dh]
                s_new = _dot(k_next, qm_ref[:, sl])
            if cur is not None:
                m_prev = m_ref[:, sl]
                m_new = jnp.maximum(m_prev, mt_ref[:, sl])
                alpha = jnp.exp2(m_prev - m_new)
                p = jnp.exp2(s_ref[:rows_c, sl] - m_new)
                l_ref[:, sl] = alpha * l_ref[:, sl] + jnp.sum(p, axis=0, keepdims=True)
                v_cur = vT_ref[0, cur, hh * dv:(hh + 1) * dv, :rows_c]
                acc_ref[:, sl] = alpha * acc_ref[:, sl] + _dot(v_cur, p.astype(BF16))
                m_ref[:, sl] = m_new
            if nxt is not None:
                if nxt_diag:
                    kpos = lax.broadcasted_iota(jnp.int32, (rows_n, cb), 0)
                    qpos = lax.broadcasted_iota(jnp.int32, (rows_n, cb), 1) + qo
                    s_new = jnp.where(kpos <= qpos, s_new, -jnp.inf)
                s_ref[:rows_n, sl] = s_new
                mt_ref[:, sl] = jnp.max(s_new, axis=0, keepdims=True)

    def finalize(i):
        lam = lam_ref[0, 0]
        inv_l = 1.0 / l_ref[...]
        rows = pl.ds(pl.multiple_of(i * tile, tile), tile)
        for hh in range(hp):
            c1 = slice(hh * 2 * tile, hh * 2 * tile + tile)
            c2 = slice(hh * 2 * tile + tile, (hh + 1) * 2 * tile)
            o = acc_ref[:, c1] * inv_l[:, c1] - lam * (acc_ref[:, c2] * inv_l[:, c2])
            o = o * lax.rsqrt(jnp.mean(o * o, axis=0, keepdims=True) + NORM_EPS) * subln_ref[...]
            o_ref[rows, hh * dv:(hh + 1) * dv] = o.T.astype(o_ref.dtype)

    load_q(0)
    reset()
    step(None, 0, True, False)

    def q_body(i, carry):
        def kv_body(j, c):
            step(j, j + 1, False, False)
            return c

        lax.fori_loop(0, i - 1, kv_body, 0)

        @pl.when(i > 0)
        def _():
            step(i - 1, i, True, False)

        @pl.when(i < nt - 1)
        def _():
            load_q(i + 1)
            step(i, 0, False, True)

        @pl.when(i == nt - 1)
        def _():
            step(i, None, False, True)

        finalize(i)
        reset()
        return carry

    lax.fori_loop(0, nt, q_body, 0)


def _attention(lam, qT, k2d, vT, subln_col, batch, seq, tile):
    nt = seq // tile
    dv = A_V_DIM
    n = batch * seq
    hp = 4
    assert A_HEADS % hp == 0
    kern = functools.partial(_attn_kernel, tile=tile, cb=min(tile, MXU_COLS), hp=hp, nt=nt)
    lanes = hp * 2 * tile
    return pl.pallas_call(
        kern,
        out_shape=jax.ShapeDtypeStruct((n, A_HEADS * dv), BF16),
        grid=(batch, A_HEADS // hp),
        in_specs=[pl.BlockSpec(memory_space=pltpu.SMEM),
                  pl.BlockSpec((1, nt, hp * 2 * A_HEAD_DIM, tile), lambda b, h: (b, 0, h, 0)),
                  pl.BlockSpec((1, seq, hp * 2 * A_HEAD_DIM), lambda b, h: (b, 0, h)),
                  pl.BlockSpec((1, nt, hp * dv, tile), lambda b, h: (b, 0, h, 0)),
                  pl.BlockSpec((dv, 1), lambda b, h: (0, 0))],
        out_specs=pl.BlockSpec((seq, hp * dv), lambda b, h: (b, h)),
        scratch_shapes=[pltpu.VMEM((2 * A_HEAD_DIM, lanes), BF16),
                        pltpu.VMEM((1, lanes), F32),
                        pltpu.VMEM((1, lanes), F32),
                        pltpu.VMEM((dv, lanes + LANES), F32),
                        pltpu.VMEM((tile, lanes + LANES), F32),
                        pltpu.VMEM((1, lanes), F32)],
        compiler_params=pltpu.CompilerParams(dimension_semantics=("arbitrary",) * 2, vmem_limit_bytes=VMEM_LIMIT),
        name="diff_attention",
    )(lam, qT, k2d.reshape(batch, seq, -1), vT, subln_col)


def _merge_kernel(x_ref, oa_ref, ob_ref, gate_ref, wa_ref, wb_ref, wo_ref, o_ref):
    d = x_ref.shape[1]
    ya = _dot(oa_ref[...], wa_ref[...])
    yb = _dot(ob_ref[...], wb_ref[...])
    g = gate_ref[...].astype(F32)
    merged = _sigmoid(g[:, :d]) * ya + _sigmoid(g[:, d:]) * yb
    o_ref[...] = x_ref[...] + _dot(merged.astype(BF16), wo_ref[...])


def _merge(x2d, oa, ob, gates, wa, wb, wo, tile):
    n, d = x2d.shape
    row = lambda w: pl.BlockSpec((tile, w), lambda i: (i, 0))
    return pl.pallas_call(
        _merge_kernel,
        out_shape=jax.ShapeDtypeStruct((n, d), F32),
        grid=(n // tile,),
        in_specs=[row(d), row(oa.shape[1]), row(ob.shape[1]), row(2 * d),
                  _const_spec(wa.shape), _const_spec(wb.shape), _const_spec(wo.shape)],
        out_specs=row(d),
        compiler_params=pltpu.CompilerParams(dimension_semantics=("arbitrary",), vmem_limit_bytes=VMEM_LIMIT),
        name="merge_outproj",
    )(x2d, oa, ob, gates, wa.astype(BF16), wb.astype(BF16), wo.astype(BF16))


def _ffn_kernel(x_ref, nw_ref, wup_ref, cw_ref, cb_ref, wdown_ref, nf_ref, o_ref, ucar_ref, act_ref, *, tiles_per_seq,
                fchunk, apply_final):
    i = pl.program_id(0)
    tile, d = x_ref.shape
    dff = wdown_ref.shape[0]

    @pl.when(i % tiles_per_seq == 0)
    def _():
        ucar_ref[...] = jnp.zeros(ucar_ref.shape, F32)

    x = x_ref[...]
    h = (x * lax.rsqrt(jnp.mean(x * x, axis=-1, keepdims=True) + NORM_EPS) * nw_ref[...]).astype(BF16)

    def up(c):
        return [_dot(h, wup_ref[:, base + c * fchunk:base + (c + 1) * fchunk]) for base in (0, dff)]

    u_next = up(0)
    for c in range(dff // fchunk):
        u_cur = u_next
        if c + 1 < dff // fchunk:
            u_next = up(c + 1)
        halves = []
        for u, base in zip(u_cur, (0, dff)):
            sl = slice(base + c * fchunk, base + (c + 1) * fchunk)
            uext = jnp.concatenate([ucar_ref[:, sl], u], axis=0)
            cw = cw_ref[:, sl]
            y = cb_ref[:, sl] + cw[FFN_CONV_WIDTH - 1:FFN_CONV_WIDTH] * u
            for s in range(1, FFN_CONV_WIDTH):
                y = y + cw[FFN_CONV_WIDTH - 1 - s:FFN_CONV_WIDTH - s] * pltpu.roll(uext, s, axis=0)[HALO:]
            ucar_ref[:, sl] = u[tile - HALO:, :]
            halves.append(y)
        act_ref[:, c * fchunk:(c + 1) * fchunk] = (_silu(halves[0]) * halves[1]).astype(BF16)
    x2 = x + _dot(act_ref[...], wdown_ref[...])
    if apply_final:
        x2 = x2 * lax.rsqrt(jnp.mean(x2 * x2, axis=-1, keepdims=True) + NORM_EPS) * nf_ref[...]
    o_ref[...] = x2


def _ffn(x2d, nw, wup, cw, cb, wdown, nf, seq, tile, fchunk, apply_final):
    n, d = x2d.shape
    dff = wdown.shape[0]
    row = pl.BlockSpec((tile, d), lambda i: (i, 0))
    kern = functools.partial(_ffn_kernel, tiles_per_seq=seq // tile, fchunk=fchunk, apply_final=apply_final)
    return pl.pallas_call(
        kern,
        out_shape=jax.ShapeDtypeStruct((n, d), F32),
        grid=(n // tile,),
        in_specs=[row, _const_spec((1, d)), _const_spec(wup.shape), _const_spec(cw.shape),
                  _const_spec((1, 2 * dff)), _const_spec(wdown.shape), _const_spec((1, d))],
        out_specs=row,
        scratch_shapes=[pltpu.VMEM((HALO, 2 * dff), F32), pltpu.VMEM((tile, dff), BF16)],
        compiler_params=pltpu.CompilerParams(dimension_semantics=("arbitrary",), vmem_limit_bytes=VMEM_LIMIT),
        name="convffn",
    )(x2d, nw, wup.astype(BF16), cw, cb, wdown.astype(BF16), nf)


def _gdn_kernel(qkv_ref, z_ref, ba_ref, alog_ref, dtb_ref, onorm_ref, o_ref,
                state_ref, lhs_ref, kf_ref, dec_ref, x0_ref, qeg_ref, kdec_ref, egl_ref, *, nb):
    c = GDN_CHUNK
    dk = B_HEAD_DIM
    bw = B_HEADS * dk
    chains = range(nb * B_HEADS)
    prepared = (lhs_ref, kf_ref, dec_ref, x0_ref, qeg_ref, kdec_ref, egl_ref)

    @pl.when(pl.program_id(1) == 0)
    def _():
        state_ref[...] = jnp.zeros(state_ref.shape, F32)
        for ref in prepared:
            ref[...] = jnp.zeros(ref.shape, ref.dtype)

    r = lax.broadcasted_iota(jnp.int32, (c, c), 0)
    q_ = lax.broadcasted_iota(jnp.int32, (c, c), 1)
    lower = r >= q_
    strict = r > q_
    tri_lo = lower.astype(BF16)
    softplus = lambda t: jnp.maximum(t, 0.0) + jnp.log1p(jnp.exp(-jnp.abs(t)))
    nt = (((1,), (1,)), ((), ()))
    tn = (((0,), (0,)), ((), ()))

    lhs = [lhs_ref[i] for i in chains]
    kfs = [kf_ref[i] for i in chains]
    decays = [dec_ref[i] for i in chains]
    xs = [x0_ref[i] for i in chains]
    qeg = [qeg_ref[i] for i in chains]
    kdec = [kdec_ref[i] for i in chains]
    egl = [egl_ref[i][0:1, :] for i in chains]

    for s_i in range(nb):
        qkv = qkv_ref[s_i].astype(F32)
        ba = ba_ref[s_i]
        beta = _sigmoid(ba)
        g = -jnp.exp(alog_ref[...]) * softplus(ba + dtb_ref[...])
        g_hi = g.astype(BF16)
        rest = g - g_hi.astype(F32)
        g_mid = rest.astype(BF16)
        g_lo = (rest - g_mid.astype(F32)).astype(BF16)
        gc = _dot(tri_lo, g_hi) + _dot(tri_lo, g_mid) + _dot(tri_lo, g_lo)
        gcT = gc.T
        for h in range(B_HEADS):
            i = s_i * B_HEADS + h
            qh = qkv[:, h * dk:(h + 1) * dk]
            kh = qkv[:, bw + h * dk:bw + (h + 1) * dk]
            vh = qkv[:, 2 * bw + h * dk:2 * bw + (h + 1) * dk]
            qn = qh * (lax.rsqrt(jnp.sum(qh * qh, axis=-1, keepdims=True) + NORM_EPS) * (dk ** -0.5))
            kn = kh * lax.rsqrt(jnp.sum(kh * kh, axis=-1, keepdims=True) + NORM_EPS)
            gl = B_HEADS + h
            bh = beta[:, h:h + 1]
            gch = gc[:, gl:gl + 1]
            glast = gc[c - 1:c, gl:gl + 1]
            diff = gch - gcT[gl:gl + 1, :]
            eg = jnp.exp(gch)
            kb = kn * bh
            lhs_ref[i] = jnp.concatenate([kb, qn], axis=0).astype(BF16)
            kf_ref[i] = kn.astype(BF16)
            dec_ref[i] = jnp.where(lower, jnp.exp(jnp.where(lower, diff, 0.0)), 0.0)
            x0_ref[i] = jnp.concatenate([vh * bh, kb * eg], axis=1)
            qeg_ref[i] = (qn * eg).astype(BF16)
            kdec_ref[i] = (kn * jnp.exp(glast - gch)).astype(BF16)
            egl_ref[i] = jnp.broadcast_to(jnp.exp(glast), egl_ref.shape[1:])

    kq = [lax.dot_general(lhs[i], kfs[i], nt, preferred_element_type=F32) for i in chains]
    a_mats = [jnp.where(strict, kq[i][:c] * decays[i], 0.0) for i in chains]
    qks = [jnp.where(lower, kq[i][c:] * decays[i], 0.0).astype(BF16) for i in chains]
    same = lambda n: (r // n) == (q_ // n)
    eye = (r == q_).astype(F32)
    base = same(SOLVE_BASE)
    bds = [jnp.where(base, a, 0.0) for a in a_mats]
    pws = [b.astype(BF16) for b in bds]
    ts = [eye - b for b in bds]
    for _ in range(int(math.log2(SOLVE_BASE)) - 1):
        pw32 = [_dot(p, p) for p in pws]
        pws = [p.astype(BF16) for p in pw32]
        ts = [_dot(ts[i].astype(BF16), (eye + pw32[i]).astype(BF16)) for i in chains]
    size = SOLVE_BASE
    while size < c:
        sub = same(2 * size) & ((r // size) % 2 == 1) & ((q_ // size) % 2 == 0)
        tb = [t.astype(BF16) for t in ts]
        lt = [_dot(jnp.where(sub, a_mats[i], 0.0).astype(BF16), tb[i]).astype(BF16) for i in chains]
        ts = [ts[i] - _dot(tb[i], lt[i]) for i in chains]
        size *= 2
    xs = [_dot(ts[i].astype(BF16), xs[i].astype(BF16)) for i in chains]
    states = [state_ref[i] for i in chains]
    wq = [jnp.concatenate([xs[i][:, dk:].astype(BF16), qeg[i]], axis=0) for i in chains]
    ws = [_dot(wq[i], states[i].astype(BF16)) for i in chains]
    v_new = [(xs[i][:, :dk] - ws[i][:c]).astype(BF16) for i in chains]
    for i in chains:
        state_ref[i] = states[i] * egl[i] + lax.dot_general(kdec[i], v_new[i], tn, preferred_element_type=F32)
    outs = [ws[i][c:] + _dot(qks[i], v_new[i]) for i in chains]
    onorm = onorm_ref[...]
    for s_i in range(nb):
        z = z_ref[s_i].astype(F32)
        for h in range(B_HEADS):
            o = outs[s_i * B_HEADS + h]
            o = o * lax.rsqrt(jnp.mean(o * o, axis=-1, keepdims=True) + NORM_EPS) * onorm
            o_ref[s_i, :, h * dk:(h + 1) * dk] = (o * _silu(z[:, h * dk:(h + 1) * dk])).astype(o_ref.dtype)


def _gdn(qkv, z, ba, a_log, dt_bias, onorm, batch, seq):
    c = GDN_CHUNK
    dk = B_HEAD_DIM
    bw = B_HEADS * dk
    nb = 2 if batch % 2 == 0 else 1
    nchunk = seq // c
    nch = nb * B_HEADS
    pad = lambda v: jnp.pad(v.astype(F32), (B_HEADS, LANES - 2 * B_HEADS))
    alog, dtb = pad(a_log), pad(dt_bias)
    prep = lambda w: pl.BlockSpec((nb, c, w), lambda b, t: (b, jnp.minimum(t, nchunk - 1), 0))
    done = lambda w: pl.BlockSpec((nb, c, w), lambda b, t: (b, jnp.maximum(t - 1, 0), 0))
    out = pl.pallas_call(
        functools.partial(_gdn_kernel, nb=nb),
        out_shape=jax.ShapeDtypeStruct((batch, seq, bw), BF16),
        grid=(batch // nb, nchunk + 1),
        in_specs=[prep(3 * bw), done(bw), prep(LANES),
                  _const_spec((1, LANES)), _const_spec((1, LANES)), _const_spec((1, dk))],
        out_specs=done(bw),
        scratch_shapes=[pltpu.VMEM((nch, dk, dk), F32),
                        pltpu.VMEM((nch, 2 * c, dk), BF16),
                        pltpu.VMEM((nch, c, dk), BF16),
                        pltpu.VMEM((nch, c, c), F32),
                        pltpu.VMEM((nch, c, 2 * dk), F32),
                        pltpu.VMEM((nch, c, dk), BF16),
                        pltpu.VMEM((nch, c, dk), BF16),
                        pltpu.VMEM((nch, HALO, LANES), F32)],
        compiler_params=pltpu.CompilerParams(dimension_semantics=("arbitrary", "arbitrary"),
                                             vmem_limit_bytes=VMEM_LIMIT),
        name="gated_deltanet",
    )(qkv.reshape(batch, seq, -1), z.reshape(batch, seq, -1), ba.reshape(batch, seq, -1),
      alog.reshape(1, -1), dtb.reshape(1, -1), onorm.reshape(1, -1))
    return out.reshape(batch * seq, bw)


def _lam_kernel(q1_ref, k1_ref, q2_ref, k2_ref, o_ref, *, lambda_init):
    s1 = jnp.sum(q1_ref[...] * k1_ref[...], axis=-1, keepdims=True)
    s2 = jnp.sum(q2_ref[...] * k2_ref[...], axis=-1, keepdims=True)
    o_ref[...] = jnp.exp(s1) - jnp.exp(s2) + lambda_init


def _lambda(q1, k1, q2, k2, lambda_init):
    spec = pl.BlockSpec((1, q1.shape[-1]), lambda: (0, 0))
    return pl.pallas_call(
        functools.partial(_lam_kernel, lambda_init=lambda_init),
        out_shape=jax.ShapeDtypeStruct((1, 1), F32),
        in_specs=[spec] * 4,
        out_specs=pl.BlockSpec((1, 1), lambda: (0, 0)),
        name="lambda",
    )(q1.reshape(1, -1), k1.reshape(1, -1), q2.reshape(1, -1), k2.reshape(1, -1))


def _pick_tile(seq, pref):
    t = min(seq, pref)
    assert seq % t == 0
    return t


def kernel(x, positions, norm_mix, w_in, lambda_q1, lambda_k1, lambda_q2, lambda_k2, a_subln, w_a_out, conv_qkv,
           a_log, dt_bias, b_onorm, w_b_out, w_o, norm_ffn, w_up, ffn_conv, ffn_conv_bias, w_down, norm_final):
    batch, seq, d = x.shape
    depth = w_in.shape[0]
    tile = _pick_tile(seq, 512)
    x2d = x.reshape(batch * seq, d)
    tables = _rope_tables(positions, tile)
    for layer in range(depth):
        lambda_init = 0.8 - 0.6 * math.exp(-0.3 * layer)
        qT, k2d, vT, qkv, z, ba, gates = _inproj(x2d, norm_mix[layer].reshape(1, d), tables, w_in[layer],
                                                 conv_qkv[layer], batch, seq, tile)
        lam = _lambda(lambda_q1[layer], lambda_k1[layer], lambda_q2[layer], lambda_k2[layer], lambda_init)
        subln = (a_subln[layer] * (1.0 - lambda_init)).reshape(-1, 1)
        oa = _attention(lam, qT, k2d, vT, subln, batch, seq, tile)
        ob = _gdn(qkv, z, ba, a_log[layer], dt_bias[layer], b_onorm[layer], batch, seq)
        x2d = _merge(x2d, oa, ob, gates, w_a_out[layer], w_b_out[layer], w_o[layer], tile)
        dff = w_down.shape[1]
        x2d = _ffn(x2d, norm_ffn[layer].reshape(1, d), w_up[layer], ffn_conv[layer],
                   ffn_conv_bias[layer].reshape(1, -1), w_down[layer], norm_final.reshape(1, d), seq, tile,
                   fchunk=256 if dff % 256 == 0 else LANES, apply_final=layer == depth - 1)
    return x2d.reshape(batch, seq, d)
```

```python
import functools
import math

import jax
import jax.numpy as jnp
from jax import lax
from jax.experimental import pallas as pl
from jax.experimental.pallas import tpu as pltpu

NORM_EPS = 1e-6
ROPE_THETA = 10000.0
LANES = 128
MXU_COLS = 256
LOG2E = math.log2(math.e)
VMEM_LIMIT = 56 * 1024 * 1024

A_HEADS = 8
A_HEAD_DIM = 64
A_V_DIM = 128
B_HEADS = 8
B_HEAD_DIM = 128
CONV_WIDTH = 4
FFN_CONV_WIDTH = 3
GDN_CHUNK = 128
SOLVE_BASE = 8
HALO = 8
F32 = jnp.float32
BF16 = jnp.bfloat16


def _dot(a, b):
    return jnp.dot(a, b, preferred_element_type=F32)


def _const_spec(shape):
    nd = len(shape)
    return pl.BlockSpec(shape, lambda *_: (0,) * nd, pipeline_mode=pl.Buffered(1))


def _sigmoid(x):
    return 1.0 / (1.0 + jnp.exp(-x))


def _silu(x):
    return x * _sigmoid(x)


def _rope_kernel(pos_ref, freq_ref, cos_ref, sin_ref):
    ang = freq_ref[...] * pos_ref[...].astype(F32)
    cos_ref[...] = jnp.cos(ang)
    sin_ref[...] = jnp.sin(ang)


def _rope_tables(positions, tile):
    n = positions.size
    half = A_HEAD_DIM // 2
    inv_freq = ROPE_THETA ** (-jnp.arange(0, A_HEAD_DIM, 2, dtype=F32) / A_HEAD_DIM)
    col = pl.BlockSpec((half, tile), lambda i: (0, i))
    return pl.pallas_call(
        _rope_kernel,
        out_shape=(jax.ShapeDtypeStruct((half, n), F32),) * 2,
        grid=(n // tile,),
        in_specs=[pl.BlockSpec((1, tile), lambda i: (0, i)), pl.BlockSpec((half, 1), lambda i: (0, 0))],
        out_specs=(col, col),
        name="rope_tables",
    )(positions.reshape(1, n), inv_freq.reshape(half, 1))


def _rotary_t(y, cos_t, sin_t, scale):
    half = A_HEAD_DIM // 2
    outs = []
    for g in range(y.shape[0] // A_HEAD_DIM):
        x1 = y[g * A_HEAD_DIM:g * A_HEAD_DIM + half]
        x2 = y[g * A_HEAD_DIM + half:(g + 1) * A_HEAD_DIM]
        outs.append((x1 * cos_t - x2 * sin_t) * scale)
        outs.append((x2 * cos_t + x1 * sin_t) * scale)
    return jnp.concatenate(outs, axis=0)


def _dot_nt(a, b):
    return lax.dot_general(a, b, (((1,), (1,)), ((), ())), preferred_element_type=F32)


def _inproj_kernel(x_ref, nw_ref, cosT_ref, sinT_ref, wqT_ref, wkT_ref, wvT_ref, wpre_ref, wz_ref, wba_ref,
                   wgate_ref, cw_ref, qT_ref, k_ref, vT_ref, qkv_ref, z_ref, ba_ref, gate_ref, carry_ref, *,
                   tiles_per_seq, chunk):
    tile = x_ref.shape[0]

    @pl.when(pl.program_id(0) % tiles_per_seq == 0)
    def _():
        carry_ref[...] = jnp.zeros(carry_ref.shape, F32)

    x = x_ref[...]
    h = (x * lax.rsqrt(jnp.mean(x * x, axis=-1, keepdims=True) + NORM_EPS) * nw_ref[...]).astype(BF16)
    ba_ref[...] = _dot(h, wba_ref[...])

    cos_t, sin_t = cosT_ref[...], sinT_ref[...]

    def q_job(sl):
        def fin(u):
            qT_ref[0, 0, sl, :] = _rotary_t(u, cos_t, sin_t, A_HEAD_DIM ** -0.5 * LOG2E).astype(BF16)
        return (lambda: _dot_nt(wqT_ref[sl, :], h)), fin

    def v_job(sl):
        def fin(u):
            vT_ref[0, 0, sl, :] = u.astype(BF16)
        return (lambda: _dot_nt(wvT_ref[sl, :], h)), fin

    def k_job(sl):
        def fin(u):
            k_ref[:, sl] = _rotary_t(u, cos_t, sin_t, 1.0).T.astype(BF16)
        return (lambda: _dot_nt(wkT_ref[sl, :], h)), fin

    def plain_job(w_ref, o_ref, sl):
        def fin(u):
            o_ref[:, sl] = u.astype(o_ref.dtype)
        return (lambda: _dot(h, w_ref[:, sl])), fin

    def conv_job(sl):
        def fin(u):
            ext = jnp.concatenate([carry_ref[:, sl], u], axis=0)
            cw = cw_ref[:, sl]
            y = cw[CONV_WIDTH - 1:CONV_WIDTH] * u
            for s in range(1, CONV_WIDTH):
                y = y + cw[CONV_WIDTH - 1 - s:CONV_WIDTH - s] * pltpu.roll(ext, s, axis=0)[HALO:]
            carry_ref[:, sl] = u[tile - HALO:, :]
            qkv_ref[:, sl] = _silu(y).astype(qkv_ref.dtype)
        return (lambda: _dot(h, wpre_ref[:, sl])), fin

    chunks = lambda width: [slice(c * chunk, (c + 1) * chunk) for c in range(width // chunk)]
    light = ([q_job(sl) for sl in chunks(wqT_ref.shape[0])] + [k_job(sl) for sl in chunks(wkT_ref.shape[0])]
             + [v_job(sl) for sl in chunks(wvT_ref.shape[0])]
             + [plain_job(wz_ref, z_ref, sl) for sl in chunks(wz_ref.shape[1])]
             + [plain_job(wgate_ref, gate_ref, sl) for sl in chunks(wgate_ref.shape[1])])
    heavy = [conv_job(sl) for sl in chunks(wpre_ref.shape[1])]
    per = max(1, len(light) // len(heavy))
    jobs = []
    for n, hv in enumerate(heavy):
        jobs += light[n * per:(n + 1) * per] + [hv]
    jobs += light[len(heavy) * per:]
    u_next = jobs[0][0]()
    for n, (_, fin) in enumerate(jobs):
        u = u_next
        if n + 1 < len(jobs):
            u_next = jobs[n + 1][0]()
        fin(u)


def _inproj(x2d, nw, tables, w_in, conv_w, batch, seq, tile):
    n, d = x2d.shape
    nt = seq // tile
    aw = A_HEADS * 2 * A_HEAD_DIM
    bw = B_HEADS * B_HEAD_DIM
    half = A_HEAD_DIM // 2
    cos_t, sin_t = tables
    wb = w_in.astype(BF16)
    o = 0
    wq_t = wb[:, o:o + aw].T; o += aw
    wk_t = wb[:, o:o + aw].T; o += aw
    wv_t = wb[:, o:o + aw].T; o += aw
    wpre = wb[:, o:o + 3 * bw]; o += 3 * bw
    wz = wb[:, o:o + bw]; o += bw
    wba = jnp.pad(wb[:, o:o + 2 * B_HEADS], ((0, 0), (0, LANES - 2 * B_HEADS))); o += 2 * B_HEADS
    wgate = wb[:, o:o + 2 * d]
    row = lambda w: pl.BlockSpec((tile, w), lambda i: (i, 0))
    col = pl.BlockSpec((half, tile), lambda i: (0, i))
    tspec = pl.BlockSpec((1, 1, aw, tile), lambda i: (i // nt, i % nt, 0, 0))
    chunk = MXU_COLS if bw % MXU_COLS == 0 and d % MXU_COLS == 0 else LANES
    return pl.pallas_call(
        functools.partial(_inproj_kernel, tiles_per_seq=nt, chunk=chunk),
        out_shape=(jax.ShapeDtypeStruct((batch, nt, aw, tile), BF16),
                   jax.ShapeDtypeStruct((n, aw), BF16),
                   jax.ShapeDtypeStruct((batch, nt, aw, tile), BF16),
                   jax.ShapeDtypeStruct((n, 3 * bw), BF16),
                   jax.ShapeDtypeStruct((n, bw), BF16),
                   jax.ShapeDtypeStruct((n, LANES), F32),
                   jax.ShapeDtypeStruct((n, 2 * d), BF16)),
        grid=(n // tile,),
        in_specs=[row(d), _const_spec((1, d)), col, col,
                  _const_spec(wq_t.shape), _const_spec(wk_t.shape), _const_spec(wv_t.shape),
                  _const_spec(wpre.shape), _const_spec(wz.shape), _const_spec(wba.shape),
                  _const_spec(wgate.shape), _const_spec(conv_w.shape)],
        out_specs=(tspec, row(aw), tspec, row(3 * bw), row(bw), row(LANES), row(2 * d)),
        scratch_shapes=[pltpu.VMEM((HALO, 3 * bw), F32)],
        compiler_params=pltpu.CompilerParams(dimension_semantics=("arbitrary",), vmem_limit_bytes=VMEM_LIMIT),
        name="inproj",
    )(x2d, nw, cos_t, sin_t, wq_t, wk_t, wv_t, wpre, wz, wba, wgate, conv_w)


def _attn_kernel(lam_ref, qT_ref, k_ref, vT_ref, subln_ref, o_ref, qm_ref, m_ref, l_ref, acc_ref, s_ref, mt_ref, *,
                 tile, cb, hp, nt):
    dh = A_HEAD_DIM
    dv = A_V_DIM
    ncb = 2 * tile // cb
    zero = jnp.zeros((dh, tile), BF16)

    def load_q(i):
        for hh in range(hp):
            qT = qT_ref[0, i, hh * 2 * dh:(hh + 1) * 2 * dh, :]
            qm_ref[:, hh * 2 * tile:hh * 2 * tile + tile] = jnp.concatenate([qT[:dh], zero], axis=0)
            qm_ref[:, hh * 2 * tile + tile:(hh + 1) * 2 * tile] = jnp.concatenate([zero, qT[dh:]], axis=0)

    def reset(first):
        m_ref[...] = jnp.full(m_ref.shape, -jnp.inf, F32)
        if first:
            l_ref[...] = jnp.zeros(l_ref.shape, F32)
            acc_ref[...] = jnp.zeros(acc_ref.shape, F32)

    def step(cur, nxt, nxt_diag, cur_diag):
        for c in range(hp * ncb):
            hh = c // ncb
            sl = slice(c * cb, (c + 1) * cb)
            qo = (c * cb) % tile
            rows_n = min(tile, qo + cb) if nxt_diag else tile
            rows_c = min(tile, qo + cb) if cur_diag else tile
            if nxt is not None:
                k_next = k_ref[0, pl.ds(pl.multiple_of(nxt * tile, tile), rows_n), hh * 2 * dh:(hh + 1) * 2 * dh]
                s_new = _dot(k_next, qm_ref[:, sl])
            if cur is not None:
                m_prev = m_ref[:, sl]
                m_new = jnp.maximum(m_prev, mt_ref[:, sl])
                alpha = jnp.exp2(m_prev - m_new)
                p = jnp.exp2(s_ref[:rows_c, sl] - m_new)
                l_ref[:, sl] = alpha * l_ref[:, sl] + jnp.sum(p, axis=0, keepdims=True)
                v_cur = vT_ref[0, cur, hh * dv:(hh + 1) * dv, :rows_c]
                acc_ref[:, sl] = alpha * acc_ref[:, sl] + _dot(v_cur, p.astype(BF16))
                m_ref[:, sl] = m_new
            if nxt is not None:
                if nxt_diag:
                    kpos = lax.broadcasted_iota(jnp.int32, (rows_n, cb), 0)
                    qpos = lax.broadcasted_iota(jnp.int32, (rows_n, cb), 1) + qo
                    s_new = jnp.where(kpos <= qpos, s_new, -jnp.inf)
                s_ref[:rows_n, sl] = s_new
                mt_ref[:, sl] = jnp.max(s_new, axis=0, keepdims=True)

    def finalize(i):
        lam = lam_ref[0, 0]
        inv_l = 1.0 / l_ref[...]
        rows = pl.ds(pl.multiple_of(i * tile, tile), tile)
        for hh in range(hp):
            c1 = slice(hh * 2 * tile, hh * 2 * tile + tile)
            c2 = slice(hh * 2 * tile + tile, (hh + 1) * 2 * tile)
            o = acc_ref[:, c1] * inv_l[:, c1] - lam * (acc_ref[:, c2] * inv_l[:, c2])
            o = o * lax.rsqrt(jnp.mean(o * o, axis=0, keepdims=True) + NORM_EPS) * subln_ref[...]
            o_ref[rows, hh * dv:(hh + 1) * dv] = o.T.astype(o_ref.dtype)

    load_q(0)
    reset(True)
    step(None, 0, True, False)

    def q_body(i, carry):
        def kv_body(j, c):
            step(j, j + 1, False, False)
            return c

        lax.fori_loop(0, i - 1, kv_body, 0)

        @pl.when(i > 0)
        def _():
            step(i - 1, i, True, False)

        @pl.when(i < nt - 1)
        def _():
            load_q(i + 1)
            step(i, 0, False, True)

        @pl.when(i == nt - 1)
        def _():
            step(i, None, False, True)

        finalize(i)
        reset(False)
        return carry

    lax.fori_loop(0, nt, q_body, 0)


def _attention(lam, qT, k2d, vT, subln_col, batch, seq, tile):
    nt = seq // tile
    dv = A_V_DIM
    n = batch * seq
    hp = 4
    assert A_HEADS % hp == 0
    kern = functools.partial(_attn_kernel, tile=tile, cb=min(tile, MXU_COLS), hp=hp, nt=nt)
    lanes = hp * 2 * tile
    return pl.pallas_call(
        kern,
        out_shape=jax.ShapeDtypeStruct((n, A_HEADS * dv), BF16),
        grid=(batch, A_HEADS // hp),
        in_specs=[pl.BlockSpec(memory_space=pltpu.SMEM),
                  pl.BlockSpec((1, nt, hp * 2 * A_HEAD_DIM, tile), lambda b, h: (b, 0, h, 0)),
                  pl.BlockSpec((1, seq, hp * 2 * A_HEAD_DIM), lambda b, h: (b, 0, h)),
                  pl.BlockSpec((1, nt, hp * dv, tile), lambda b, h: (b, 0, h, 0)),
                  pl.BlockSpec((dv, 1), lambda b, h: (0, 0))],
        out_specs=pl.BlockSpec((seq, hp * dv), lambda b, h: (b, h)),
        scratch_shapes=[pltpu.VMEM((2 * A_HEAD_DIM, lanes), BF16),
                        pltpu.VMEM((1, lanes), F32),
                        pltpu.VMEM((1, lanes), F32),
                        pltpu.VMEM((dv, lanes + LANES), F32),
                        pltpu.VMEM((tile, lanes + LANES), F32),
                        pltpu.VMEM((1, lanes), F32)],
        compiler_params=pltpu.CompilerParams(dimension_semantics=("arbitrary",) * 2, vmem_limit_bytes=VMEM_LIMIT),
        name="diff_attention",
    )(lam, qT, k2d.reshape(batch, seq, -1), vT, subln_col)


def _merge_kernel(x_ref, oa_ref, ob_ref, gate_ref, wa_ref, wb_ref, wo_ref, o_ref):
    d = x_ref.shape[1]
    ya = _dot(oa_ref[...], wa_ref[...])
    yb = _dot(ob_ref[...], wb_ref[...])
    g = gate_ref[...].astype(F32)
    merged = _sigmoid(g[:, :d]) * ya + _sigmoid(g[:, d:]) * yb
    o_ref[...] = x_ref[...] + _dot(merged.astype(BF16), wo_ref[...])


def _merge(x2d, oa, ob, gates, wa, wb, wo, tile):
    n, d = x2d.shape
    row = lambda w: pl.BlockSpec((tile, w), lambda i: (i, 0))
    return pl.pallas_call(
        _merge_kernel,
        out_shape=jax.ShapeDtypeStruct((n, d), F32),
        grid=(n // tile,),
        in_specs=[row(d), row(oa.shape[1]), row(ob.shape[1]), row(2 * d),
                  _const_spec(wa.shape), _const_spec(wb.shape), _const_spec(wo.shape)],
        out_specs=row(d),
        compiler_params=pltpu.CompilerParams(dimension_semantics=("arbitrary",), vmem_limit_bytes=VMEM_LIMIT),
        name="merge_outproj",
    )(x2d, oa, ob, gates, wa.astype(BF16), wb.astype(BF16), wo.astype(BF16))


def _ffn_kernel(x_ref, nw_ref, wup_ref, cw_ref, cb_ref, wdown_ref, nf_ref, o_ref, ucar_ref, act_ref, *, tiles_per_seq,
                fchunk, apply_final):
    i = pl.program_id(0)
    tile, d = x_ref.shape
    dff = wdown_ref.shape[0]

    @pl.when(i % tiles_per_seq == 0)
    def _():
        ucar_ref[...] = jnp.zeros(ucar_ref.shape, F32)

    x = x_ref[...]
    h = (x * lax.rsqrt(jnp.mean(x * x, axis=-1, keepdims=True) + NORM_EPS) * nw_ref[...]).astype(BF16)

    def up(c):
        return [_dot(h, wup_ref[:, base + c * fchunk:base + (c + 1) * fchunk]) for base in (0, dff)]

    u_next = up(0)
    for c in range(dff // fchunk):
        u_cur = u_next
        if c + 1 < dff // fchunk:
            u_next = up(c + 1)
        halves = []
        for u, base in zip(u_cur, (0, dff)):
            sl = slice(base + c * fchunk, base + (c + 1) * fchunk)
            uext = jnp.concatenate([ucar_ref[:, sl], u], axis=0)
            cw = cw_ref[:, sl]
            y = cb_ref[:, sl] + cw[FFN_CONV_WIDTH - 1:FFN_CONV_WIDTH] * u
            for s in range(1, FFN_CONV_WIDTH):
                y = y + cw[FFN_CONV_WIDTH - 1 - s:FFN_CONV_WIDTH - s] * pltpu.roll(uext, s, axis=0)[HALO:]
            ucar_ref[:, sl] = u[tile - HALO:, :]
            halves.append(y)
        act_ref[:, c * fchunk:(c + 1) * fchunk] = (_silu(halves[0]) * halves[1]).astype(BF16)
    x2 = x + _dot(act_ref[...], wdown_ref[...])
    if apply_final:
        x2 = x2 * lax.rsqrt(jnp.mean(x2 * x2, axis=-1, keepdims=True) + NORM_EPS) * nf_ref[...]
    o_ref[...] = x2


def _ffn(x2d, nw, wup, cw, cb, wdown, nf, seq, tile, fchunk, apply_final):
    n, d = x2d.shape
    dff = wdown.shape[0]
    row = pl.BlockSpec((tile, d), lambda i: (i, 0))
    kern = functools.partial(_ffn_kernel, tiles_per_seq=seq // tile, fchunk=fchunk, apply_final=apply_final)
    return pl.pallas_call(
        kern,
        out_shape=jax.ShapeDtypeStruct((n, d), F32),
        grid=(n // tile,),
        in_specs=[row, _const_spec((1, d)), _const_spec(wup.shape), _const_spec(cw.shape),
                  _const_spec((1, 2 * dff)), _const_spec(wdown.shape), _const_spec((1, d))],
        out_specs=row,
        scratch_shapes=[pltpu.VMEM((HALO, 2 * dff), F32), pltpu.VMEM((tile, dff), BF16)],
        compiler_params=pltpu.CompilerParams(dimension_semantics=("arbitrary",), vmem_limit_bytes=VMEM_LIMIT),
        name="convffn",
    )(x2d, nw, wup.astype(BF16), cw, cb, wdown.astype(BF16), nf)


def _gdn_kernel(qkv_ref, z_ref, ba_ref, alog_ref, dtb_ref, onorm_ref, o_ref,
                state_ref, lhs_ref, kf_ref, dec_ref, x0_ref, qeg_ref, kdec_ref, egl_ref, *, nb):
    c = GDN_CHUNK
    dk = B_HEAD_DIM
    bw = B_HEADS * dk
    chains = range(nb * B_HEADS)
    prepared = (lhs_ref, kf_ref, dec_ref, x0_ref, qeg_ref, kdec_ref, egl_ref)

    @pl.when(pl.program_id(1) == 0)
    def _():
        state_ref[...] = jnp.zeros(state_ref.shape, F32)
        for ref in prepared:
            ref[...] = jnp.zeros(ref.shape, ref.dtype)

    r = lax.broadcasted_iota(jnp.int32, (c, c), 0)
    q_ = lax.broadcasted_iota(jnp.int32, (c, c), 1)
    lower = r >= q_
    strict = r > q_
    tri_lo = lower.astype(BF16)
    softplus = lambda t: jnp.maximum(t, 0.0) + jnp.log1p(jnp.exp(-jnp.abs(t)))
    nt = (((1,), (1,)), ((), ()))
    tn = (((0,), (0,)), ((), ()))

    lhs = [lhs_ref[i] for i in chains]
    kfs = [kf_ref[i] for i in chains]
    decays = [dec_ref[i] for i in chains]
    xs = [x0_ref[i] for i in chains]
    qeg = [qeg_ref[i] for i in chains]
    kdec = [kdec_ref[i] for i in chains]
    egl = [egl_ref[i][0:1, :] for i in chains]

    for s_i in range(nb):
        qkv = qkv_ref[s_i].astype(F32)
        ba = ba_ref[s_i]
        beta = _sigmoid(ba)
        g = -jnp.exp(alog_ref[...]) * softplus(ba + dtb_ref[...])
        g_hi = g.astype(BF16)
        rest = g - g_hi.astype(F32)
        g_mid = rest.astype(BF16)
        g_lo = (rest - g_mid.astype(F32)).astype(BF16)
        gc = _dot(tri_lo, g_hi) + _dot(tri_lo, g_mid) + _dot(tri_lo, g_lo)
        gcT = gc.T
        for h in range(B_HEADS):
            i = s_i * B_HEADS + h
            qh = qkv[:, h * dk:(h + 1) * dk]
            kh = qkv[:, bw + h * dk:bw + (h + 1) * dk]
            vh = qkv[:, 2 * bw + h * dk:2 * bw + (h + 1) * dk]
            qn = qh * (lax.rsqrt(jnp.sum(qh * qh, axis=-1, keepdims=True) + NORM_EPS) * (dk ** -0.5))
            kn = kh * lax.rsqrt(jnp.sum(kh * kh, axis=-1, keepdims=True) + NORM_EPS)
            gl = B_HEADS + h
            bh = beta[:, h:h + 1]
            gch = gc[:, gl:gl + 1]
            glast = gc[c - 1:c, gl:gl + 1]
            diff = gch - gcT[gl:gl + 1, :]
            eg = jnp.exp(gch)
            kb = kn * bh
            lhs_ref[i] = jnp.concatenate([kb, qn], axis=0).astype(BF16)
            kf_ref[i] = kn.astype(BF16)
            dec_ref[i] = jnp.where(lower, jnp.exp(jnp.where(lower, diff, 0.0)), 0.0)
            x0_ref[i] = jnp.concatenate([vh * bh, kb * eg], axis=1)
            qeg_ref[i] = (qn * eg).astype(BF16)
            kdec_ref[i] = (kn * jnp.exp(glast - gch)).astype(BF16)
            egl_ref[i] = jnp.broadcast_to(jnp.exp(glast), egl_ref.shape[1:])

    kq = [lax.dot_general(lhs[i], kfs[i], nt, preferred_element_type=F32) for i in chains]
    a_mats = [jnp.where(strict, kq[i][:c] * decays[i], 0.0) for i in chains]
    qks = [jnp.where(lower, kq[i][c:] * decays[i], 0.0).astype(BF16) for i in chains]
    same = lambda n: (r // n) == (q_ // n)
    eye = (r == q_).astype(F32)
    base = same(SOLVE_BASE)
    bds = [jnp.where(base, a, 0.0) for a in a_mats]
    pws = [b.astype(BF16) for b in bds]
    ts = [eye - b for b in bds]
    for _ in range(int(math.log2(SOLVE_BASE)) - 1):
        pw32 = [_dot(p, p) for p in pws]
        pws = [p.astype(BF16) for p in pw32]
        ts = [_dot(ts[i].astype(BF16), (eye + pw32[i]).astype(BF16)) for i in chains]
    size = SOLVE_BASE
    while size < c:
        sub = same(2 * size) & ((r // size) % 2 == 1) & ((q_ // size) % 2 == 0)
        tb = [t.astype(BF16) for t in ts]
        lt = [_dot(jnp.where(sub, a_mats[i], 0.0).astype(BF16), tb[i]).astype(BF16) for i in chains]
        ts = [ts[i] - _dot(tb[i], lt[i]) for i in chains]
        size *= 2
    xs = [_dot(ts[i].astype(BF16), xs[i].astype(BF16)) for i in chains]
    states = [state_ref[i] for i in chains]
    wq = [jnp.concatenate([xs[i][:, dk:].astype(BF16), qeg[i]], axis=0) for i in chains]
    ws = [_dot(wq[i], states[i].astype(BF16)) for i in chains]
    v_new = [(xs[i][:, :dk] - ws[i][:c]).astype(BF16) for i in chains]
    for i in chains:
        state_ref[i] = states[i] * egl[i] + lax.dot_general(kdec[i], v_new[i], tn, preferred_element_type=F32)
    outs = [ws[i][c:] + _dot(qks[i], v_new[i]) for i in chains]
    onorm = onorm_ref[...]
    for s_i in range(nb):
        z = z_ref[s_i].astype(F32)
        for h in range(B_HEADS):
            o = outs[s_i * B_HEADS + h]
            o = o * lax.rsqrt(jnp.mean(o * o, axis=-1, keepdims=True) + NORM_EPS) * onorm
            o_ref[s_i, :, h * dk:(h + 1) * dk] = (o * _silu(z[:, h * dk:(h + 1) * dk])).astype(o_ref.dtype)


def _gdn(qkv, z, ba, a_log, dt_bias, onorm, batch, seq):
    c = GDN_CHUNK
    dk = B_HEAD_DIM
    bw = B_HEADS * dk
    nb = 2 if batch % 2 == 0 else 1
    nchunk = seq // c
    nch = nb * B_HEADS
    pad = lambda v: jnp.pad(v.astype(F32), (B_HEADS, LANES - 2 * B_HEADS))
    alog, dtb = pad(a_log), pad(dt_bias)
    prep = lambda w: pl.BlockSpec((nb, c, w), lambda b, t: (b, jnp.minimum(t, nchunk - 1), 0))
    done = lambda w: pl.BlockSpec((nb, c, w), lambda b, t: (b, jnp.maximum(t - 1, 0), 0))
    out = pl.pallas_call(
        functools.partial(_gdn_kernel, nb=nb),
        out_shape=jax.ShapeDtypeStruct((batch, seq, bw), BF16),
        grid=(batch // nb, nchunk + 1),
        in_specs=[prep(3 * bw), done(bw), prep(LANES),
                  _const_spec((1, LANES)), _const_spec((1, LANES)), _const_spec((1, dk))],
        out_specs=done(bw),
        scratch_shapes=[pltpu.VMEM((nch, dk, dk), F32),
                        pltpu.VMEM((nch, 2 * c, dk), BF16),
                        pltpu.VMEM((nch, c, dk), BF16),
                        pltpu.VMEM((nch, c, c), F32),
                        pltpu.VMEM((nch, c, 2 * dk), F32),
                        pltpu.VMEM((nch, c, dk), BF16),
                        pltpu.VMEM((nch, c, dk), BF16),
                        pltpu.VMEM((nch, HALO, LANES), F32)],
        compiler_params=pltpu.CompilerParams(dimension_semantics=("arbitrary", "arbitrary"),
                                             vmem_limit_bytes=VMEM_LIMIT),
        name="gated_deltanet",
    )(qkv.reshape(batch, seq, -1), z.reshape(batch, seq, -1), ba.reshape(batch, seq, -1),
      alog.reshape(1, -1), dtb.reshape(1, -1), onorm.reshape(1, -1))
    return out.reshape(batch * seq, bw)


def _lam_kernel(q1_ref, k1_ref, q2_ref, k2_ref, o_ref, *, lambda_init):
    s1 = jnp.sum(q1_ref[...] * k1_ref[...], axis=-1, keepdims=True)
    s2 = jnp.sum(q2_ref[...] * k2_ref[...], axis=-1, keepdims=True)
    o_ref[...] = jnp.exp(s1) - jnp.exp(s2) + lambda_init


def _lambda(q1, k1, q2, k2, lambda_init):
    spec = pl.BlockSpec((1, q1.shape[-1]), lambda: (0, 0))
    return pl.pallas_call(
        functools.partial(_lam_kernel, lambda_init=lambda_init),
        out_shape=jax.ShapeDtypeStruct((1, 1), F32),
        in_specs=[spec] * 4,
        out_specs=pl.BlockSpec((1, 1), lambda: (0, 0)),
        name="lambda",
    )(q1.reshape(1, -1), k1.reshape(1, -1), q2.reshape(1, -1), k2.reshape(1, -1))


def _pick_tile(seq, pref):
    t = min(seq, pref)
    assert seq % t == 0
    return t


def kernel(x, positions, norm_mix, w_in, lambda_q1, lambda_k1, lambda_q2, lambda_k2, a_subln, w_a_out, conv_qkv,
           a_log, dt_bias, b_onorm, w_b_out, w_o, norm_ffn, w_up, ffn_conv, ffn_conv_bias, w_down, norm_final):
    batch, seq, d = x.shape
    depth = w_in.shape[0]
    tile = _pick_tile(seq, 512)
    x2d = x.reshape(batch * seq, d)
    tables = _rope_tables(positions, tile)
    for layer in range(depth):
        lambda_init = 0.8 - 0.6 * math.exp(-0.3 * layer)
        qT, k2d, vT, qkv, z, ba, gates = _inproj(x2d, norm_mix[layer].reshape(1, d), tables, w_in[layer],
                                                 conv_qkv[layer], batch, seq, tile)
        lam = _lambda(lambda_q1[layer], lambda_k1[layer], lambda_q2[layer], lambda_k2[layer], lambda_init)
        subln = (a_subln[layer] * (1.0 - lambda_init)).reshape(-1, 1)
        oa = _attention(lam, qT, k2d, vT, subln, batch, seq, tile)
        ob = _gdn(qkv, z, ba, a_log[layer], dt_bias[layer], b_onorm[layer], batch, seq)
        x2d = _merge(x2d, oa, ob, gates, w_a_out[layer], w_b_out[layer], w_o[layer], tile)
        dff = w_down.shape[1]
        x2d = _ffn(x2d, norm_ffn[layer].reshape(1, d), w_up[layer], ffn_conv[layer],
                   ffn_conv_bias[layer].reshape(1, -1), w_down[layer], norm_final.reshape(1, d), seq, tile,
                   fchunk=256 if dff % 256 == 0 else LANES, apply_final=layer == depth - 1)
    return x2d.reshape(batch, seq, d)
```

```python
import functools
import math

import jax
import jax.numpy as jnp
from jax import lax
from jax.experimental import pallas as pl
from jax.experimental.pallas import tpu as pltpu

NORM_EPS = 1e-6
ROPE_THETA = 10000.0
LANES = 128
MXU_COLS = 256
LOG2E = math.log2(math.e)
VMEM_LIMIT = 56 * 1024 * 1024
ROW_TILE = 512

A_HEADS = 8
A_HEAD_DIM = 64
A_V_DIM = 128
B_HEADS = 8
B_HEAD_DIM = 128
CONV_WIDTH = 4
FFN_CONV_WIDTH = 3
GDN_CHUNK = 128
SOLVE_BASE = 8
HALO = 8
F32 = jnp.float32
BF16 = jnp.bfloat16


def _dot(a, b):
    return jnp.dot(a, b, preferred_element_type=F32)


def _const_spec(shape):
    nd = len(shape)
    return pl.BlockSpec(shape, lambda *_: (0,) * nd, pipeline_mode=pl.Buffered(1))


def _sigmoid(x):
    return 1.0 / (1.0 + jnp.exp(-x))


def _silu(x):
    return x * _sigmoid(x)


def _rope_kernel(pos_ref, freq_ref, cos_ref, sin_ref):
    ang = freq_ref[...] * pos_ref[...].astype(F32)
    cos_ref[...] = jnp.cos(ang)
    sin_ref[...] = jnp.sin(ang)


def _rope_tables(positions, tile):
    n = positions.size
    half = A_HEAD_DIM // 2
    inv_freq = ROPE_THETA ** (-jnp.arange(0, A_HEAD_DIM, 2, dtype=F32) / A_HEAD_DIM)
    col = pl.BlockSpec((half, tile), lambda i: (0, i))
    return pl.pallas_call(
        _rope_kernel,
        out_shape=(jax.ShapeDtypeStruct((half, n), F32),) * 2,
        grid=(n // tile,),
        in_specs=[pl.BlockSpec((1, tile), lambda i: (0, i)), pl.BlockSpec((half, 1), lambda i: (0, 0))],
        out_specs=(col, col),
        name="rope_tables",
    )(positions.reshape(1, n), inv_freq.reshape(half, 1))


def _rotary_t(y, cos_t, sin_t, scale):
    half = A_HEAD_DIM // 2
    outs = []
    for g in range(y.shape[0] // A_HEAD_DIM):
        x1 = y[g * A_HEAD_DIM:g * A_HEAD_DIM + half]
        x2 = y[g * A_HEAD_DIM + half:(g + 1) * A_HEAD_DIM]
        outs.append((x1 * cos_t - x2 * sin_t) * scale)
        outs.append((x2 * cos_t + x1 * sin_t) * scale)
    return jnp.concatenate(outs, axis=0)


def _dot_nt(a, b):
    return lax.dot_general(a, b, (((1,), (1,)), ((), ())), preferred_element_type=F32)


def _inproj_kernel(x_ref, nw_ref, cosT_ref, sinT_ref, wqT_ref, wkT_ref, wvT_ref, wpre_ref, wz_ref, wba_ref,
                   wgate_ref, cw_ref, qT_ref, k_ref, vT_ref, qkv_ref, z_ref, ba_ref, gate_ref, carry_ref, *,
                   tiles_per_seq, chunk):
    tile = x_ref.shape[0]

    @pl.when(pl.program_id(0) % tiles_per_seq == 0)
    def _():
        carry_ref[...] = jnp.zeros(carry_ref.shape, F32)

    x = x_ref[...]
    h = (x * lax.rsqrt(jnp.mean(x * x, axis=-1, keepdims=True) + NORM_EPS) * nw_ref[...]).astype(BF16)
    ba_ref[...] = _dot(h, wba_ref[...])

    cos_t, sin_t = cosT_ref[...], sinT_ref[...]

    def q_job(sl):
        def fin(u):
            qT_ref[0, 0, sl, :] = _rotary_t(u, cos_t, sin_t, A_HEAD_DIM ** -0.5 * LOG2E).astype(BF16)
        return (lambda: _dot_nt(wqT_ref[sl, :], h)), fin

    def v_job(sl):
        def fin(u):
            vT_ref[0, 0, sl, :] = u.astype(BF16)
        return (lambda: _dot_nt(wvT_ref[sl, :], h)), fin

    def k_job(sl):
        def fin(u):
            k_ref[:, sl] = _rotary_t(u, cos_t, sin_t, 1.0).T.astype(BF16)
        return (lambda: _dot_nt(wkT_ref[sl, :], h)), fin

    def plain_job(w_ref, o_ref, sl):
        def fin(u):
            o_ref[:, sl] = u.astype(o_ref.dtype)
        return (lambda: _dot(h, w_ref[:, sl])), fin

    def conv_job(sl):
        def fin(u):
            ext = jnp.concatenate([carry_ref[:, sl], u], axis=0)
            cw = cw_ref[:, sl]
            y = cw[CONV_WIDTH - 1:CONV_WIDTH] * u
            for s in range(1, CONV_WIDTH):
                y = y + cw[CONV_WIDTH - 1 - s:CONV_WIDTH - s] * pltpu.roll(ext, s, axis=0)[HALO:]
            carry_ref[:, sl] = u[tile - HALO:, :]
            qkv_ref[:, sl] = _silu(y).astype(qkv_ref.dtype)
        return (lambda: _dot(h, wpre_ref[:, sl])), fin

    chunks = lambda width: [slice(c * chunk, (c + 1) * chunk) for c in range(width // chunk)]
    light = ([q_job(sl) for sl in chunks(wqT_ref.shape[0])] + [k_job(sl) for sl in chunks(wkT_ref.shape[0])]
             + [v_job(sl) for sl in chunks(wvT_ref.shape[0])]
             + [plain_job(wz_ref, z_ref, sl) for sl in chunks(wz_ref.shape[1])]
             + [plain_job(wgate_ref, gate_ref, sl) for sl in chunks(wgate_ref.shape[1])])
    heavy = [conv_job(sl) for sl in chunks(wpre_ref.shape[1])]
    per = max(1, len(light) // len(heavy))
    jobs = []
    for n, hv in enumerate(heavy):
        jobs += light[n * per:(n + 1) * per] + [hv]
    jobs += light[len(heavy) * per:]
    u_next = jobs[0][0]()
    for n, (_, fin) in enumerate(jobs):
        u = u_next
        if n + 1 < len(jobs):
            u_next = jobs[n + 1][0]()
        fin(u)


def _inproj(x2d, nw, tables, w_in, conv_w, batch, seq, tile):
    n, d = x2d.shape
    nt = seq // tile
    aw = A_HEADS * 2 * A_HEAD_DIM
    bw = B_HEADS * B_HEAD_DIM
    half = A_HEAD_DIM // 2
    cos_t, sin_t = tables
    wb = w_in.astype(BF16)
    o = 0
    wq_t = wb[:, o:o + aw].T; o += aw
    wk_t = wb[:, o:o + aw].T; o += aw
    wv_t = wb[:, o:o + aw].T; o += aw
    wpre = wb[:, o:o + 3 * bw]; o += 3 * bw
    wz = wb[:, o:o + bw]; o += bw
    wba = jnp.pad(wb[:, o:o + 2 * B_HEADS], ((0, 0), (0, LANES - 2 * B_HEADS))); o += 2 * B_HEADS
    wgate = wb[:, o:o + 2 * d]
    row = lambda w: pl.BlockSpec((tile, w), lambda i: (i, 0))
    col = pl.BlockSpec((half, tile), lambda i: (0, i))
    tspec = pl.BlockSpec((1, 1, aw, tile), lambda i: (i // nt, i % nt, 0, 0))
    chunk = MXU_COLS if bw % MXU_COLS == 0 and d % MXU_COLS == 0 else LANES
    return pl.pallas_call(
        functools.partial(_inproj_kernel, tiles_per_seq=nt, chunk=chunk),
        out_shape=(jax.ShapeDtypeStruct((batch, nt, aw, tile), BF16),
                   jax.ShapeDtypeStruct((n, aw), BF16),
                   jax.ShapeDtypeStruct((batch, nt, aw, tile), BF16),
                   jax.ShapeDtypeStruct((n, 3 * bw), BF16),
                   jax.ShapeDtypeStruct((n, bw), BF16),
                   jax.ShapeDtypeStruct((n, LANES), F32),
                   jax.ShapeDtypeStruct((n, 2 * d), BF16)),
        grid=(n // tile,),
        in_specs=[row(d), _const_spec((1, d)), col, col,
                  _const_spec(wq_t.shape), _const_spec(wk_t.shape), _const_spec(wv_t.shape),
                  _const_spec(wpre.shape), _const_spec(wz.shape), _const_spec(wba.shape),
                  _const_spec(wgate.shape), _const_spec(conv_w.shape)],
        out_specs=(tspec, row(aw), tspec, row(3 * bw), row(bw), row(LANES), row(2 * d)),
        scratch_shapes=[pltpu.VMEM((HALO, 3 * bw), F32)],
        compiler_params=pltpu.CompilerParams(dimension_semantics=("arbitrary",), vmem_limit_bytes=VMEM_LIMIT),
        name="inproj",
    )(x2d, nw, cos_t, sin_t, wq_t, wk_t, wv_t, wpre, wz, wba, wgate, conv_w)


def _attn_kernel(lam_ref, qT_ref, k_ref, vT_ref, subln_ref, o_ref, qm_ref, m_ref, l_ref, acc_ref, s_ref, mt_ref, *,
                 tile, cb, hp, nt):
    dh = A_HEAD_DIM
    dv = A_V_DIM
    ncb = 2 * tile // cb
    zero = jnp.zeros((dh, tile), BF16)

    def load_q(i):
        for hh in range(hp):
            qT = qT_ref[0, i, hh * 2 * dh:(hh + 1) * 2 * dh, :]
            qm_ref[:, hh * 2 * tile:hh * 2 * tile + tile] = jnp.concatenate([qT[:dh], zero], axis=0)
            qm_ref[:, hh * 2 * tile + tile:(hh + 1) * 2 * tile] = jnp.concatenate([zero, qT[dh:]], axis=0)

    def reset(first):
        m_ref[...] = jnp.full(m_ref.shape, -jnp.inf, F32)
        if first:
            l_ref[...] = jnp.zeros(l_ref.shape, F32)
            acc_ref[...] = jnp.zeros(acc_ref.shape, F32)

    def step(cur, nxt, nxt_diag, cur_diag):
        def rows(c, diag):
            return min(tile, (c * cb) % tile + cb) if diag else tile

        def scores(c):
            hh = c // ncb
            k_next = k_ref[0, pl.ds(pl.multiple_of(nxt * tile, tile), rows(c, nxt_diag)),
                           hh * 2 * dh:(hh + 1) * 2 * dh]
            return _dot(k_next, qm_ref[:, c * cb:(c + 1) * cb])

        s_ahead = scores(0) if nxt is not None else None
        for c in range(hp * ncb):
            hh = c // ncb
            sl = slice(c * cb, (c + 1) * cb)
            qo = (c * cb) % tile
            rows_n, rows_c = rows(c, nxt_diag), rows(c, cur_diag)
            s_new = s_ahead
            if nxt is not None and c + 1 < hp * ncb:
                s_ahead = scores(c + 1)
            if cur is not None:
                m_prev = m_ref[:, sl]
                m_new = jnp.maximum(m_prev, mt_ref[:, sl])
                alpha = jnp.exp2(m_prev - m_new)
                p = jnp.exp2(s_ref[:rows_c, sl] - m_new)
                l_ref[:, sl] = alpha * l_ref[:, sl] + jnp.sum(p, axis=0, keepdims=True)
                v_cur = vT_ref[0, cur, hh * dv:(hh + 1) * dv, :rows_c]
                acc_ref[:, sl] = alpha * acc_ref[:, sl] + _dot(v_cur, p.astype(BF16))
                m_ref[:, sl] = m_new
            if nxt is not None:
                if nxt_diag:
                    kpos = lax.broadcasted_iota(jnp.int32, (rows_n, cb), 0)
                    qpos = lax.broadcasted_iota(jnp.int32, (rows_n, cb), 1) + qo
                    s_new = jnp.where(kpos <= qpos, s_new, -jnp.inf)
                s_ref[:rows_n, sl] = s_new
                mt_ref[:, sl] = jnp.max(s_new, axis=0, keepdims=True)

    def finalize(i):
        lam = lam_ref[0, 0]
        inv_l = 1.0 / l_ref[...]
        rows = pl.ds(pl.multiple_of(i * tile, tile), tile)
        for hh in range(hp):
            c1 = slice(hh * 2 * tile, hh * 2 * tile + tile)
            c2 = slice(hh * 2 * tile + tile, (hh + 1) * 2 * tile)
            o = acc_ref[:, c1] * inv_l[:, c1] - lam * (acc_ref[:, c2] * inv_l[:, c2])
            o = o * lax.rsqrt(jnp.mean(o * o, axis=0, keepdims=True) + NORM_EPS) * subln_ref[...]
            o_ref[rows, hh * dv:(hh + 1) * dv] = o.T.astype(o_ref.dtype)

    load_q(0)
    reset(True)
    step(None, 0, True, False)

    def q_body(i, carry):
        def kv_body(j, c):
            step(j, j + 1, False, False)
            return c

        lax.fori_loop(0, i - 1, kv_body, 0)

        @pl.when(i > 0)
        def _():
            step(i - 1, i, True, False)

        @pl.when(i < nt - 1)
        def _():
            load_q(i + 1)
            step(i, 0, False, True)

        @pl.when(i == nt - 1)
        def _():
            step(i, None, False, True)

        finalize(i)
        reset(False)
        return carry

    lax.fori_loop(0, nt, q_body, 0)


def _attention(lam, qT, k2d, vT, subln_col, batch, seq, tile):
    nt = seq // tile
    dv = A_V_DIM
    n = batch * seq
    hp = 4
    assert A_HEADS % hp == 0
    kern = functools.partial(_attn_kernel, tile=tile, cb=min(tile, MXU_COLS), hp=hp, nt=nt)
    lanes = hp * 2 * tile
    return pl.pallas_call(
        kern,
        out_shape=jax.ShapeDtypeStruct((n, A_HEADS * dv), BF16),
        grid=(batch, A_HEADS // hp),
        in_specs=[pl.BlockSpec(memory_space=pltpu.SMEM),
                  pl.BlockSpec((1, nt, hp * 2 * A_HEAD_DIM, tile), lambda b, h: (b, 0, h, 0)),
                  pl.BlockSpec((1, seq, hp * 2 * A_HEAD_DIM), lambda b, h: (b, 0, h)),
                  pl.BlockSpec((1, nt, hp * dv, tile), lambda b, h: (b, 0, h, 0)),
                  pl.BlockSpec((dv, 1), lambda b, h: (0, 0))],
        out_specs=pl.BlockSpec((seq, hp * dv), lambda b, h: (b, h)),
        scratch_shapes=[pltpu.VMEM((2 * A_HEAD_DIM, lanes), BF16),
                        pltpu.VMEM((1, lanes), F32),
                        pltpu.VMEM((1, lanes), F32),
                        pltpu.VMEM((dv, lanes + LANES), F32),
                        pltpu.VMEM((tile, lanes + LANES), F32),
                        pltpu.VMEM((1, lanes), F32)],
        compiler_params=pltpu.CompilerParams(dimension_semantics=("arbitrary",) * 2, vmem_limit_bytes=VMEM_LIMIT),
        name="diff_attention",
    )(lam, qT, k2d.reshape(batch, seq, -1), vT, subln_col)


def _merge_kernel(x_ref, oa_ref, ob_ref, gate_ref, wa_ref, wb_ref, wo_ref, o_ref):
    d = x_ref.shape[1]
    ya = _dot(oa_ref[...], wa_ref[...])
    yb = _dot(ob_ref[...], wb_ref[...])
    g = gate_ref[...].astype(F32)
    merged = _sigmoid(g[:, :d]) * ya + _sigmoid(g[:, d:]) * yb
    o_ref[...] = x_ref[...] + _dot(merged.astype(BF16), wo_ref[...])


def _merge(x2d, oa, ob, gates, wa, wb, wo, tile):
    n, d = x2d.shape
    row = lambda w: pl.BlockSpec((tile, w), lambda i: (i, 0))
    return pl.pallas_call(
        _merge_kernel,
        out_shape=jax.ShapeDtypeStruct((n, d), F32),
        grid=(n // tile,),
        in_specs=[row(d), row(oa.shape[1]), row(ob.shape[1]), row(2 * d),
                  _const_spec(wa.shape), _const_spec(wb.shape), _const_spec(wo.shape)],
        out_specs=row(d),
        compiler_params=pltpu.CompilerParams(dimension_semantics=("arbitrary",), vmem_limit_bytes=VMEM_LIMIT),
        name="merge_outproj",
    )(x2d, oa, ob, gates, wa.astype(BF16), wb.astype(BF16), wo.astype(BF16))


def _ffn_kernel(x_ref, nw_ref, wup_ref, cw_ref, cb_ref, wdown_ref, nf_ref, o_ref, ucar_ref, act_ref, *, tiles_per_seq,
                fchunk, apply_final):
    i = pl.program_id(0)
    tile, d = x_ref.shape
    dff = wdown_ref.shape[0]

    @pl.when(i % tiles_per_seq == 0)
    def _():
        ucar_ref[...] = jnp.zeros(ucar_ref.shape, F32)

    x = x_ref[...]
    h = (x * lax.rsqrt(jnp.mean(x * x, axis=-1, keepdims=True) + NORM_EPS) * nw_ref[...]).astype(BF16)

    def up(c):
        return [_dot(h, wup_ref[:, base + c * fchunk:base + (c + 1) * fchunk]) for base in (0, dff)]

    u_next = up(0)
    for c in range(dff // fchunk):
        u_cur = u_next
        if c + 1 < dff // fchunk:
            u_next = up(c + 1)
        halves = []
        for u, base in zip(u_cur, (0, dff)):
            sl = slice(base + c * fchunk, base + (c + 1) * fchunk)
            uext = jnp.concatenate([ucar_ref[:, sl], u], axis=0)
            cw = cw_ref[:, sl]
            y = cb_ref[:, sl] + cw[FFN_CONV_WIDTH - 1:FFN_CONV_WIDTH] * u
            for s in range(1, FFN_CONV_WIDTH):
                y = y + cw[FFN_CONV_WIDTH - 1 - s:FFN_CONV_WIDTH - s] * pltpu.roll(uext, s, axis=0)[HALO:]
            ucar_ref[:, sl] = u[tile - HALO:, :]
            halves.append(y)
        act_ref[:, c * fchunk:(c + 1) * fchunk] = (_silu(halves[0]) * halves[1]).astype(BF16)
    x2 = x + _dot(act_ref[...], wdown_ref[...])
    if apply_final:
        x2 = x2 * lax.rsqrt(jnp.mean(x2 * x2, axis=-1, keepdims=True) + NORM_EPS) * nf_ref[...]
    o_ref[...] = x2


def _ffn(x2d, nw, wup, cw, cb, wdown, nf, seq, tile, fchunk, apply_final):
    n, d = x2d.shape
    dff = wdown.shape[0]
    row = pl.BlockSpec((tile, d), lambda i: (i, 0))
    kern = functools.partial(_ffn_kernel, tiles_per_seq=seq // tile, fchunk=fchunk, apply_final=apply_final)
    return pl.pallas_call(
        kern,
        out_shape=jax.ShapeDtypeStruct((n, d), F32),
        grid=(n // tile,),
        in_specs=[row, _const_spec((1, d)), _const_spec(wup.shape), _const_spec(cw.shape),
                  _const_spec((1, 2 * dff)), _const_spec(wdown.shape), _const_spec((1, d))],
        out_specs=row,
        scratch_shapes=[pltpu.VMEM((HALO, 2 * dff), F32), pltpu.VMEM((tile, dff), BF16)],
        compiler_params=pltpu.CompilerParams(dimension_semantics=("arbitrary",), vmem_limit_bytes=VMEM_LIMIT),
        name="convffn",
    )(x2d, nw, wup.astype(BF16), cw, cb, wdown.astype(BF16), nf)


def _gdn_kernel(qkv_ref, z_ref, ba_ref, alog_ref, dtb_ref, onorm_ref, o_ref,
                state_ref, lhs_ref, kf_ref, dec_ref, x0_ref, qeg_ref, kdec_ref, egl_ref, *, nb):
    c = GDN_CHUNK
    dk = B_HEAD_DIM
    bw = B_HEADS * dk
    chains = range(nb * B_HEADS)
    prepared = (lhs_ref, kf_ref, dec_ref, x0_ref, qeg_ref, kdec_ref, egl_ref)

    @pl.when(pl.program_id(1) == 0)
    def _():
        state_ref[...] = jnp.zeros(state_ref.shape, F32)
        for ref in prepared:
            ref[...] = jnp.zeros(ref.shape, ref.dtype)

    r = lax.broadcasted_iota(jnp.int32, (c, c), 0)
    q_ = lax.broadcasted_iota(jnp.int32, (c, c), 1)
    lower = r >= q_
    strict = r > q_
    tri_lo = lower.astype(BF16)
    softplus = lambda t: jnp.maximum(t, 0.0) + jnp.log1p(jnp.exp(-jnp.abs(t)))
    nt = (((1,), (1,)), ((), ()))
    tn = (((0,), (0,)), ((), ()))

    lhs = [lhs_ref[i] for i in chains]
    kfs = [kf_ref[i] for i in chains]
    decays = [dec_ref[i] for i in chains]
    xs = [x0_ref[i] for i in chains]
    qeg = [qeg_ref[i] for i in chains]
    kdec = [kdec_ref[i] for i in chains]
    egl = [egl_ref[i][0:1, :] for i in chains]

    for s_i in range(nb):
        qkv = qkv_ref[s_i].astype(F32)
        ba = ba_ref[s_i]
        beta = _sigmoid(ba)
        g = -jnp.exp(alog_ref[...]) * softplus(ba + dtb_ref[...])
        g_hi = g.astype(BF16)
        rest = g - g_hi.astype(F32)
        g_mid = rest.astype(BF16)
        g_lo = (rest - g_mid.astype(F32)).astype(BF16)
        gc = _dot(tri_lo, g_hi) + _dot(tri_lo, g_mid) + _dot(tri_lo, g_lo)
        gcT = gc.T
        for h in range(B_HEADS):
            i = s_i * B_HEADS + h
            qh = qkv[:, h * dk:(h + 1) * dk]
            kh = qkv[:, bw + h * dk:bw + (h + 1) * dk]
            vh = qkv[:, 2 * bw + h * dk:2 * bw + (h + 1) * dk]
            qn = qh * (lax.rsqrt(jnp.sum(qh * qh, axis=-1, keepdims=True) + NORM_EPS) * (dk ** -0.5))
            kn = kh * lax.rsqrt(jnp.sum(kh * kh, axis=-1, keepdims=True) + NORM_EPS)
            gl = B_HEADS + h
            bh = beta[:, h:h + 1]
            gch = gc[:, gl:gl + 1]
            glast = gc[c - 1:c, gl:gl + 1]
            diff = gch - gcT[gl:gl + 1, :]
            eg = jnp.exp(gch)
            kb = kn * bh
            lhs_ref[i] = jnp.concatenate([kb, qn], axis=0).astype(BF16)
            kf_ref[i] = kn.astype(BF16)
            dec_ref[i] = jnp.where(lower, jnp.exp(jnp.where(lower, diff, 0.0)), 0.0)
            x0_ref[i] = jnp.concatenate([vh * bh, kb * eg], axis=1)
            qeg_ref[i] = (qn * eg).astype(BF16)
            kdec_ref[i] = (kn * jnp.exp(glast - gch)).astype(BF16)
            egl_ref[i] = jnp.broadcast_to(jnp.exp(glast), egl_ref.shape[1:])

    kq = [lax.dot_general(lhs[i], kfs[i], nt, preferred_element_type=F32) for i in chains]
    a_mats = [jnp.where(strict, kq[i][:c] * decays[i], 0.0) for i in chains]
    qks = [jnp.where(lower, kq[i][c:] * decays[i], 0.0).astype(BF16) for i in chains]
    same = lambda n: (r // n) == (q_ // n)
    eye = (r == q_).astype(F32)
    base = same(SOLVE_BASE)
    bds = [jnp.where(base, a, 0.0) for a in a_mats]
    pws = [b.astype(BF16) for b in bds]
    ts = [eye - b for b in bds]
    for _ in range(int(math.log2(SOLVE_BASE)) - 1):
        pw32 = [_dot(p, p) for p in pws]
        pws = [p.astype(BF16) for p in pw32]
        ts = [_dot(ts[i].astype(BF16), (eye + pw32[i]).astype(BF16)) for i in chains]
    size = SOLVE_BASE
    while size < c:
        sub = same(2 * size) & ((r // size) % 2 == 1) & ((q_ // size) % 2 == 0)
        tb = [t.astype(BF16) for t in ts]
        lt = [_dot(jnp.where(sub, a_mats[i], 0.0).astype(BF16), tb[i]).astype(BF16) for i in chains]
        ts = [ts[i] - _dot(tb[i], lt[i]) for i in chains]
        size *= 2
    xs = [_dot(ts[i].astype(BF16), xs[i].astype(BF16)) for i in chains]
    states = [state_ref[i] for i in chains]
    wq = [jnp.concatenate([xs[i][:, dk:].astype(BF16), qeg[i]], axis=0) for i in chains]
    ws = [_dot(wq[i], states[i].astype(BF16)) for i in chains]
    v_new = [(xs[i][:, :dk] - ws[i][:c]).astype(BF16) for i in chains]
    for i in chains:
        state_ref[i] = states[i] * egl[i] + lax.dot_general(kdec[i], v_new[i], tn, preferred_element_type=F32)
    outs = [ws[i][c:] + _dot(qks[i], v_new[i]) for i in chains]
    onorm = onorm_ref[...]
    for s_i in range(nb):
        z = z_ref[s_i].astype(F32)
        for h in range(B_HEADS):
            o = outs[s_i * B_HEADS + h]
            o = o * lax.rsqrt(jnp.mean(o * o, axis=-1, keepdims=True) + NORM_EPS) * onorm
            o_ref[s_i, :, h * dk:(h + 1) * dk] = (o * _silu(z[:, h * dk:(h + 1) * dk])).astype(o_ref.dtype)


def _gdn(qkv, z, ba, a_log, dt_bias, onorm, batch, seq):
    c = GDN_CHUNK
    dk = B_HEAD_DIM
    bw = B_HEADS * dk
    nb = 2 if batch % 2 == 0 else 1
    nchunk = seq // c
    nch = nb * B_HEADS
    pad = lambda v: jnp.pad(v.astype(F32), (B_HEADS, LANES - 2 * B_HEADS))
    alog, dtb = pad(a_log), pad(dt_bias)
    prep = lambda w: pl.BlockSpec((nb, c, w), lambda b, t: (b, jnp.minimum(t, nchunk - 1), 0))
    done = lambda w: pl.BlockSpec((nb, c, w), lambda b, t: (b, jnp.maximum(t - 1, 0), 0))
    out = pl.pallas_call(
        functools.partial(_gdn_kernel, nb=nb),
        out_shape=jax.ShapeDtypeStruct((batch, seq, bw), BF16),
        grid=(batch // nb, nchunk + 1),
        in_specs=[prep(3 * bw), done(bw), prep(LANES),
                  _const_spec((1, LANES)), _const_spec((1, LANES)), _const_spec((1, dk))],
        out_specs=done(bw),
        scratch_shapes=[pltpu.VMEM((nch, dk, dk), F32),
                        pltpu.VMEM((nch, 2 * c, dk), BF16),
                        pltpu.VMEM((nch, c, dk), BF16),
                        pltpu.VMEM((nch, c, c), F32),
                        pltpu.VMEM((nch, c, 2 * dk), F32),
                        pltpu.VMEM((nch, c, dk), BF16),
                        pltpu.VMEM((nch, c, dk), BF16),
                        pltpu.VMEM((nch, HALO, LANES), F32)],
        compiler_params=pltpu.CompilerParams(dimension_semantics=("arbitrary", "arbitrary"),
                                             vmem_limit_bytes=VMEM_LIMIT),
        name="gated_deltanet",
    )(qkv.reshape(batch, seq, -1), z.reshape(batch, seq, -1), ba.reshape(batch, seq, -1),
      alog.reshape(1, -1), dtb.reshape(1, -1), onorm.reshape(1, -1))
    return out.reshape(batch * seq, bw)


def _lam_kernel(q1_ref, k1_ref, q2_ref, k2_ref, o_ref, *, lambda_init):
    s1 = jnp.sum(q1_ref[...] * k1_ref[...], axis=-1, keepdims=True)
    s2 = jnp.sum(q2_ref[...] * k2_ref[...], axis=-1, keepdims=True)
    o_ref[...] = jnp.exp(s1) - jnp.exp(s2) + lambda_init


def _lambda(q1, k1, q2, k2, lambda_init):
    spec = pl.BlockSpec((1, q1.shape[-1]), lambda: (0, 0))
    return pl.pallas_call(
        functools.partial(_lam_kernel, lambda_init=lambda_init),
        out_shape=jax.ShapeDtypeStruct((1, 1), F32),
        in_specs=[spec] * 4,
        out_specs=pl.BlockSpec((1, 1), lambda: (0, 0)),
        name="lambda",
    )(q1.reshape(1, -1), k1.reshape(1, -1), q2.reshape(1, -1), k2.reshape(1, -1))


def _pick_tile(seq, pref):
    t = min(seq, pref)
    assert seq % t == 0
    return t


def kernel(x, positions, norm_mix, w_in, lambda_q1, lambda_k1, lambda_q2, lambda_k2, a_subln, w_a_out, conv_qkv,
           a_log, dt_bias, b_onorm, w_b_out, w_o, norm_ffn, w_up, ffn_conv, ffn_conv_bias, w_down, norm_final):
    batch, seq, d = x.shape
    depth = w_in.shape[0]
    tile = _pick_tile(seq, ROW_TILE)
    x2d = x.reshape(batch * seq, d)
    tables = _rope_tables(positions, tile)
    for layer in range(depth):
        lambda_init = 0.8 - 0.6 * math.exp(-0.3 * layer)
        qT, k2d, vT, qkv, z, ba, gates = _inproj(x2d, norm_mix[layer].reshape(1, d), tables, w_in[layer],
                                                 conv_qkv[layer], batch, seq, tile)
        lam = _lambda(lambda_q1[layer], lambda_k1[layer], lambda_q2[layer], lambda_k2[layer], lambda_init)
        subln = (a_subln[layer] * (1.0 - lambda_init)).reshape(-1, 1)
        oa = _attention(lam, qT, k2d, vT, subln, batch, seq, tile)
        ob = _gdn(qkv, z, ba, a_log[layer], dt_bias[layer], b_onorm[layer], batch, seq)
        x2d = _merge(x2d, oa, ob, gates, w_a_out[layer], w_b_out[layer], w_o[layer], tile)
        dff = w_down.shape[1]
        x2d = _ffn(x2d, norm_ffn[layer].reshape(1, d), w_up[layer], ffn_conv[layer],
                   ffn_conv_bias[layer].reshape(1, -1), w_down[layer], norm_final.reshape(1, d), seq, tile,
                   fchunk=MXU_COLS if dff % MXU_COLS == 0 else LANES, apply_final=layer == depth - 1)
    return x2d.reshape(batch, seq, d)
```

```python
import functools
import math

import jax
import jax.numpy as jnp
from jax import lax
from jax.experimental import pallas as pl
from jax.experimental.pallas import tpu as pltpu

NORM_EPS = 1e-6
ROPE_THETA = 10000.0
LANES = 128
MXU_COLS = 256
LOG2E = math.log2(math.e)
VMEM_LIMIT = 56 * 1024 * 1024
ROW_TILE = 512

A_HEADS = 8
A_HEAD_DIM = 64
A_V_DIM = 128
B_HEADS = 8
B_HEAD_DIM = 128
CONV_WIDTH = 4
FFN_CONV_WIDTH = 3
GDN_CHUNK = 128
CONV_SPLIT = 2
SOLVE_BASE = 8
HALO = 8
F32 = jnp.float32
BF16 = jnp.bfloat16


def _dot(a, b):
    return jnp.dot(a, b, preferred_element_type=F32)


def _const_spec(shape):
    nd = len(shape)
    return pl.BlockSpec(shape, lambda *_: (0,) * nd, pipeline_mode=pl.Buffered(1))


def _sigmoid(x):
    return 1.0 / (1.0 + jnp.exp(-x))


def _silu(x):
    return x * _sigmoid(x)


def _rope_kernel(pos_ref, freq_ref, cos_ref, sin_ref):
    ang = freq_ref[...] * pos_ref[...].astype(F32)
    cos_ref[...] = jnp.cos(ang)
    sin_ref[...] = jnp.sin(ang)


def _rope_tables(positions, tile):
    n = positions.size
    half = A_HEAD_DIM // 2
    inv_freq = ROPE_THETA ** (-jnp.arange(0, A_HEAD_DIM, 2, dtype=F32) / A_HEAD_DIM)
    col = pl.BlockSpec((half, tile), lambda i: (0, i))
    return pl.pallas_call(
        _rope_kernel,
        out_shape=(jax.ShapeDtypeStruct((half, n), F32),) * 2,
        grid=(n // tile,),
        in_specs=[pl.BlockSpec((1, tile), lambda i: (0, i)), pl.BlockSpec((half, 1), lambda i: (0, 0))],
        out_specs=(col, col),
        name="rope_tables",
    )(positions.reshape(1, n), inv_freq.reshape(half, 1))


def _rotary_t(y, cos_t, sin_t, scale):
    half = A_HEAD_DIM // 2
    outs = []
    for g in range(y.shape[0] // A_HEAD_DIM):
        x1 = y[g * A_HEAD_DIM:g * A_HEAD_DIM + half]
        x2 = y[g * A_HEAD_DIM + half:(g + 1) * A_HEAD_DIM]
        outs.append((x1 * cos_t - x2 * sin_t) * scale)
        outs.append((x2 * cos_t + x1 * sin_t) * scale)
    return jnp.concatenate(outs, axis=0)


def _dot_nt(a, b):
    return lax.dot_general(a, b, (((1,), (1,)), ((), ())), preferred_element_type=F32)


def _inproj_kernel(x_ref, nw_ref, cosT_ref, sinT_ref, wqT_ref, wkT_ref, wvT_ref, wpre_ref, wz_ref, wba_ref,
                   wgate_ref, cw_ref, qT_ref, k_ref, vT_ref, qkv_ref, z_ref, ba_ref, gate_ref, carry_ref, *,
                   tiles_per_seq, chunk):
    tile = x_ref.shape[0]

    @pl.when(pl.program_id(0) % tiles_per_seq == 0)
    def _():
        carry_ref[...] = jnp.zeros(carry_ref.shape, F32)

    x = x_ref[...]
    h = (x * lax.rsqrt(jnp.mean(x * x, axis=-1, keepdims=True) + NORM_EPS) * nw_ref[...]).astype(BF16)
    ba_ref[...] = _dot(h, wba_ref[...])

    cos_t, sin_t = cosT_ref[...], sinT_ref[...]

    def q_job(sl):
        def fin(u):
            qT_ref[0, 0, sl, :] = _rotary_t(u, cos_t, sin_t, A_HEAD_DIM ** -0.5 * LOG2E).astype(BF16)
        return (lambda: _dot_nt(wqT_ref[sl, :], h)), fin

    def v_job(sl):
        def fin(u):
            vT_ref[0, 0, sl, :] = u.astype(BF16)
        return (lambda: _dot_nt(wvT_ref[sl, :], h)), fin

    def k_job(sl):
        def fin(u):
            k_ref[:, sl] = _rotary_t(u, cos_t, sin_t, 1.0).T.astype(BF16)
        return (lambda: _dot_nt(wkT_ref[sl, :], h)), fin

    def plain_job(w_ref, o_ref, sl):
        def fin(u):
            o_ref[:, sl] = u.astype(o_ref.dtype)
        return (lambda: _dot(h, w_ref[:, sl])), fin

    def conv_job(sl, r0, nrows):
        def fin(u):
            ext = jnp.concatenate([carry_ref[:, sl], u], axis=0)
            cw = cw_ref[:, sl]
            y = cw[CONV_WIDTH - 1:CONV_WIDTH] * u
            for s in range(1, CONV_WIDTH):
                y = y + cw[CONV_WIDTH - 1 - s:CONV_WIDTH - s] * pltpu.roll(ext, s, axis=0)[HALO:]
            carry_ref[:, sl] = u[nrows - HALO:, :]
            qkv_ref[r0:r0 + nrows, sl] = _silu(y).astype(qkv_ref.dtype)
        return (lambda: _dot(h[r0:r0 + nrows], wpre_ref[:, sl])), fin

    chunks = lambda width: [slice(c * chunk, (c + 1) * chunk) for c in range(width // chunk)]
    light = ([q_job(sl) for sl in chunks(wqT_ref.shape[0])] + [k_job(sl) for sl in chunks(wkT_ref.shape[0])]
             + [v_job(sl) for sl in chunks(wvT_ref.shape[0])]
             + [plain_job(wz_ref, z_ref, sl) for sl in chunks(wz_ref.shape[1])]
             + [plain_job(wgate_ref, gate_ref, sl) for sl in chunks(wgate_ref.shape[1])])
    heavy = [conv_job(sl, r0, tile // CONV_SPLIT) for sl in chunks(wpre_ref.shape[1])
             for r0 in range(0, tile, tile // CONV_SPLIT)]
    per = max(1, len(light) // len(heavy))
    jobs = []
    for n, hv in enumerate(heavy):
        jobs += light[n * per:(n + 1) * per] + [hv]
    jobs += light[len(heavy) * per:]
    u_next = jobs[0][0]()
    for n, (_, fin) in enumerate(jobs):
        u = u_next
        if n + 1 < len(jobs):
            u_next = jobs[n + 1][0]()
        fin(u)


def _inproj(x2d, nw, tables, w_in, conv_w, batch, seq, tile):
    n, d = x2d.shape
    nt = seq // tile
    aw = A_HEADS * 2 * A_HEAD_DIM
    bw = B_HEADS * B_HEAD_DIM
    half = A_HEAD_DIM // 2
    cos_t, sin_t = tables
    wb = w_in.astype(BF16)
    o = 0
    wq_t = wb[:, o:o + aw].T; o += aw
    wk_t = wb[:, o:o + aw].T; o += aw
    wv_t = wb[:, o:o + aw].T; o += aw
    wpre = wb[:, o:o + 3 * bw]; o += 3 * bw
    wz = wb[:, o:o + bw]; o += bw
    wba = jnp.pad(wb[:, o:o + 2 * B_HEADS], ((0, 0), (0, LANES - 2 * B_HEADS))); o += 2 * B_HEADS
    wgate = wb[:, o:o + 2 * d]
    row = lambda w: pl.BlockSpec((tile, w), lambda i: (i, 0))
    col = pl.BlockSpec((half, tile), lambda i: (0, i))
    tspec = pl.BlockSpec((1, 1, aw, tile), lambda i: (i // nt, i % nt, 0, 0))
    chunk = MXU_COLS if bw % MXU_COLS == 0 and d % MXU_COLS == 0 else LANES
    return pl.pallas_call(
        functools.partial(_inproj_kernel, tiles_per_seq=nt, chunk=chunk),
        out_shape=(jax.ShapeDtypeStruct((batch, nt, aw, tile), BF16),
                   jax.ShapeDtypeStruct((n, aw), BF16),
                   jax.ShapeDtypeStruct((batch, nt, aw, tile), BF16),
                   jax.ShapeDtypeStruct((n, 3 * bw), BF16),
                   jax.ShapeDtypeStruct((n, bw), BF16),
                   jax.ShapeDtypeStruct((n, LANES), F32),
                   jax.ShapeDtypeStruct((n, 2 * d), BF16)),
        grid=(n // tile,),
        in_specs=[row(d), _const_spec((1, d)), col, col,
                  _const_spec(wq_t.shape), _const_spec(wk_t.shape), _const_spec(wv_t.shape),
                  _const_spec(wpre.shape), _const_spec(wz.shape), _const_spec(wba.shape),
                  _const_spec(wgate.shape), _const_spec(conv_w.shape)],
        out_specs=(tspec, row(aw), tspec, row(3 * bw), row(bw), row(LANES), row(2 * d)),
        scratch_shapes=[pltpu.VMEM((HALO, 3 * bw), F32)],
        compiler_params=pltpu.CompilerParams(dimension_semantics=("arbitrary",), vmem_limit_bytes=VMEM_LIMIT),
        name="inproj",
    )(x2d, nw, cos_t, sin_t, wq_t, wk_t, wv_t, wpre, wz, wba, wgate, conv_w)


def _attn_kernel(lam_ref, qT_ref, k_ref, vT_ref, subln_ref, o_ref, qm_ref, m_ref, l_ref, acc_ref, s_ref, mt_ref, *,
                 tile, cb, hp, nt):
    dh = A_HEAD_DIM
    dv = A_V_DIM
    ncb = 2 * tile // cb
    zero = jnp.zeros((dh, tile), BF16)

    def load_q(i):
        for hh in range(hp):
            qT = qT_ref[0, i, hh * 2 * dh:(hh + 1) * 2 * dh, :]
            qm_ref[:, hh * 2 * tile:hh * 2 * tile + tile] = jnp.concatenate([qT[:dh], zero], axis=0)
            qm_ref[:, hh * 2 * tile + tile:(hh + 1) * 2 * tile] = jnp.concatenate([zero, qT[dh:]], axis=0)

    def reset(first):
        m_ref[...] = jnp.full(m_ref.shape, -jnp.inf, F32)
        if first:
            l_ref[...] = jnp.zeros(l_ref.shape, F32)
            acc_ref[...] = jnp.zeros(acc_ref.shape, F32)

    def step(cur, nxt, nxt_diag, cur_diag):
        def rows(c, diag):
            return min(tile, (c * cb) % tile + cb) if diag else tile

        def scores(c):
            hh = c // ncb
            k_next = k_ref[0, pl.ds(pl.multiple_of(nxt * tile, tile), rows(c, nxt_diag)),
                           hh * 2 * dh:(hh + 1) * 2 * dh]
            return _dot(k_next, qm_ref[:, c * cb:(c + 1) * cb])

        s_ahead = scores(0) if nxt is not None else None
        for c in range(hp * ncb):
            hh = c // ncb
            sl = slice(c * cb, (c + 1) * cb)
            qo = (c * cb) % tile
            rows_n, rows_c = rows(c, nxt_diag), rows(c, cur_diag)
            s_new = s_ahead
            if nxt is not None and c + 1 < hp * ncb:
                s_ahead = scores(c + 1)
            if cur is not None:
                m_prev = m_ref[:, sl]
                m_new = jnp.maximum(m_prev, mt_ref[:, sl])
                alpha = jnp.exp2(m_prev - m_new)
                p = jnp.exp2(s_ref[:rows_c, sl] - m_new)
                l_ref[:, sl] = alpha * l_ref[:, sl] + jnp.sum(p, axis=0, keepdims=True)
                v_cur = vT_ref[0, cur, hh * dv:(hh + 1) * dv, :rows_c]
                acc_ref[:, sl] = alpha * acc_ref[:, sl] + _dot(v_cur, p.astype(BF16))
                m_ref[:, sl] = m_new
            if nxt is not None:
                if nxt_diag:
                    kpos = lax.broadcasted_iota(jnp.int32, (rows_n, cb), 0)
                    qpos = lax.broadcasted_iota(jnp.int32, (rows_n, cb), 1) + qo
                    s_new = jnp.where(kpos <= qpos, s_new, -jnp.inf)
                s_ref[:rows_n, sl] = s_new
                mt_ref[:, sl] = jnp.max(s_new, axis=0, keepdims=True)

    def finalize(i):
        lam = lam_ref[0, 0]
        inv_l = 1.0 / l_ref[...]
        rows = pl.ds(pl.multiple_of(i * tile, tile), tile)
        for hh in range(hp):
            c1 = slice(hh * 2 * tile, hh * 2 * tile + tile)
            c2 = slice(hh * 2 * tile + tile, (hh + 1) * 2 * tile)
            o = acc_ref[:, c1] * inv_l[:, c1] - lam * (acc_ref[:, c2] * inv_l[:, c2])
            o = o * lax.rsqrt(jnp.mean(o * o, axis=0, keepdims=True) + NORM_EPS) * subln_ref[...]
            o_ref[rows, hh * dv:(hh + 1) * dv] = o.T.astype(o_ref.dtype)

    load_q(0)
    reset(True)
    step(None, 0, True, False)

    def q_body(i, carry):
        def kv_body(j, c):
            step(j, j + 1, False, False)
            return c

        lax.fori_loop(0, i - 1, kv_body, 0)

        @pl.when(i > 0)
        def _():
            step(i - 1, i, True, False)

        @pl.when(i < nt - 1)
        def _():
            load_q(i + 1)
            step(i, 0, False, True)

        @pl.when(i == nt - 1)
        def _():
            step(i, None, False, True)

        finalize(i)
        reset(False)
        return carry

    lax.fori_loop(0, nt, q_body, 0)


def _attention(lam, qT, k2d, vT, subln_col, batch, seq, tile):
    nt = seq // tile
    dv = A_V_DIM
    n = batch * seq
    hp = 4
    assert A_HEADS % hp == 0
    kern = functools.partial(_attn_kernel, tile=tile, cb=min(tile, MXU_COLS), hp=hp, nt=nt)
    lanes = hp * 2 * tile
    return pl.pallas_call(
        kern,
        out_shape=jax.ShapeDtypeStruct((n, A_HEADS * dv), BF16),
        grid=(batch, A_HEADS // hp),
        in_specs=[pl.BlockSpec(memory_space=pltpu.SMEM),
                  pl.BlockSpec((1, nt, hp * 2 * A_HEAD_DIM, tile), lambda b, h: (b, 0, h, 0)),
                  pl.BlockSpec((1, seq, hp * 2 * A_HEAD_DIM), lambda b, h: (b, 0, h)),
                  pl.BlockSpec((1, nt, hp * dv, tile), lambda b, h: (b, 0, h, 0)),
                  pl.BlockSpec((dv, 1), lambda b, h: (0, 0))],
        out_specs=pl.BlockSpec((seq, hp * dv), lambda b, h: (b, h)),
        scratch_shapes=[pltpu.VMEM((2 * A_HEAD_DIM, lanes), BF16),
                        pltpu.VMEM((1, lanes), F32),
                        pltpu.VMEM((1, lanes), F32),
                        pltpu.VMEM((dv, lanes + LANES), F32),
                        pltpu.VMEM((tile, lanes + LANES), F32),
                        pltpu.VMEM((1, lanes), F32)],
        compiler_params=pltpu.CompilerParams(dimension_semantics=("arbitrary",) * 2, vmem_limit_bytes=VMEM_LIMIT),
        name="diff_attention",
    )(lam, qT, k2d.reshape(batch, seq, -1), vT, subln_col)


def _merge_kernel(x_ref, oa_ref, ob_ref, gate_ref, wa_ref, wb_ref, wo_ref, o_ref):
    d = x_ref.shape[1]
    ya = _dot(oa_ref[...], wa_ref[...])
    yb = _dot(ob_ref[...], wb_ref[...])
    g = gate_ref[...].astype(F32)
    merged = _sigmoid(g[:, :d]) * ya + _sigmoid(g[:, d:]) * yb
    o_ref[...] = x_ref[...] + _dot(merged.astype(BF16), wo_ref[...])


def _merge(x2d, oa, ob, gates, wa, wb, wo, tile):
    n, d = x2d.shape
    row = lambda w: pl.BlockSpec((tile, w), lambda i: (i, 0))
    return pl.pallas_call(
        _merge_kernel,
        out_shape=jax.ShapeDtypeStruct((n, d), F32),
        grid=(n // tile,),
        in_specs=[row(d), row(oa.shape[1]), row(ob.shape[1]), row(2 * d),
                  _const_spec(wa.shape), _const_spec(wb.shape), _const_spec(wo.shape)],
        out_specs=row(d),
        compiler_params=pltpu.CompilerParams(dimension_semantics=("arbitrary",), vmem_limit_bytes=VMEM_LIMIT),
        name="merge_outproj",
    )(x2d, oa, ob, gates, wa.astype(BF16), wb.astype(BF16), wo.astype(BF16))


def _ffn_kernel(x_ref, nw_ref, wup_ref, cw_ref, cb_ref, wdown_ref, nf_ref, o_ref, ucar_ref, act_ref, *, tiles_per_seq,
                fchunk, apply_final):
    i = pl.program_id(0)
    tile, d = x_ref.shape
    dff = wdown_ref.shape[0]

    @pl.when(i % tiles_per_seq == 0)
    def _():
        ucar_ref[...] = jnp.zeros(ucar_ref.shape, F32)

    x = x_ref[...]
    h = (x * lax.rsqrt(jnp.mean(x * x, axis=-1, keepdims=True) + NORM_EPS) * nw_ref[...]).astype(BF16)

    def up(c):
        return [_dot(h, wup_ref[:, base + c * fchunk:base + (c + 1) * fchunk]) for base in (0, dff)]

    u_next = up(0)
    for c in range(dff // fchunk):
        u_cur = u_next
        if c + 1 < dff // fchunk:
            u_next = up(c + 1)
        halves = []
        for u, base in zip(u_cur, (0, dff)):
            sl = slice(base + c * fchunk, base + (c + 1) * fchunk)
            uext = jnp.concatenate([ucar_ref[:, sl], u], axis=0)
            cw = cw_ref[:, sl]
            y = cb_ref[:, sl] + cw[FFN_CONV_WIDTH - 1:FFN_CONV_WIDTH] * u
            for s in range(1, FFN_CONV_WIDTH):
                y = y + cw[FFN_CONV_WIDTH - 1 - s:FFN_CONV_WIDTH - s] * pltpu.roll(uext, s, axis=0)[HALO:]
            ucar_ref[:, sl] = u[tile - HALO:, :]
            halves.append(y)
        act_ref[:, c * fchunk:(c + 1) * fchunk] = (_silu(halves[0]) * halves[1]).astype(BF16)
    x2 = x + _dot(act_ref[...], wdown_ref[...])
    if apply_final:
        x2 = x2 * lax.rsqrt(jnp.mean(x2 * x2, axis=-1, keepdims=True) + NORM_EPS) * nf_ref[...]
    o_ref[...] = x2


def _ffn(x2d, nw, wup, cw, cb, wdown, nf, seq, tile, fchunk, apply_final):
    n, d = x2d.shape
    dff = wdown.shape[0]
    row = pl.BlockSpec((tile, d), lambda i: (i, 0))
    kern = functools.partial(_ffn_kernel, tiles_per_seq=seq // tile, fchunk=fchunk, apply_final=apply_final)
    return pl.pallas_call(
        kern,
        out_shape=jax.ShapeDtypeStruct((n, d), F32),
        grid=(n // tile,),
        in_specs=[row, _const_spec((1, d)), _const_spec(wup.shape), _const_spec(cw.shape),
                  _const_spec((1, 2 * dff)), _const_spec(wdown.shape), _const_spec((1, d))],
        out_specs=row,
        scratch_shapes=[pltpu.VMEM((HALO, 2 * dff), F32), pltpu.VMEM((tile, dff), BF16)],
        compiler_params=pltpu.CompilerParams(dimension_semantics=("arbitrary",), vmem_limit_bytes=VMEM_LIMIT),
        name="convffn",
    )(x2d, nw, wup.astype(BF16), cw, cb, wdown.astype(BF16), nf)


def _gdn_kernel(qkv_ref, z_ref, ba_ref, alog_ref, dtb_ref, onorm_ref, o_ref,
                state_ref, lhs_ref, kf_ref, dec_ref, x0_ref, qeg_ref, kdec_ref, egl_ref, *, nb):
    c = GDN_CHUNK
    dk = B_HEAD_DIM
    bw = B_HEADS * dk
    chains = range(nb * B_HEADS)
    prepared = (lhs_ref, kf_ref, dec_ref, x0_ref, qeg_ref, kdec_ref, egl_ref)

    @pl.when(pl.program_id(1) == 0)
    def _():
        state_ref[...] = jnp.zeros(state_ref.shape, F32)
        for ref in prepared:
            ref[...] = jnp.zeros(ref.shape, ref.dtype)

    r = lax.broadcasted_iota(jnp.int32, (c, c), 0)
    q_ = lax.broadcasted_iota(jnp.int32, (c, c), 1)
    lower = r >= q_
    strict = r > q_
    tri_lo = lower.astype(BF16)
    softplus = lambda t: jnp.maximum(t, 0.0) + jnp.log1p(jnp.exp(-jnp.abs(t)))
    nt = (((1,), (1,)), ((), ()))
    tn = (((0,), (0,)), ((), ()))

    lhs = [lhs_ref[i] for i in chains]
    kfs = [kf_ref[i] for i in chains]
    decays = [dec_ref[i] for i in chains]
    xs = [x0_ref[i] for i in chains]
    qeg = [qeg_ref[i] for i in chains]
    kdec = [kdec_ref[i] for i in chains]
    egl = [egl_ref[i][0:1, :] for i in chains]

    for s_i in range(nb):
        qkv = qkv_ref[s_i].astype(F32)
        ba = ba_ref[s_i]
        beta = _sigmoid(ba)
        g = -jnp.exp(alog_ref[...]) * softplus(ba + dtb_ref[...])
        g_hi = g.astype(BF16)
        rest = g - g_hi.astype(F32)
        g_mid = rest.astype(BF16)
        g_lo = (rest - g_mid.astype(F32)).astype(BF16)
        gc = _dot(tri_lo, g_hi) + _dot(tri_lo, g_mid) + _dot(tri_lo, g_lo)
        gcT = gc.T
        for h in range(B_HEADS):
            i = s_i * B_HEADS + h
            qh = qkv[:, h * dk:(h + 1) * dk]
            kh = qkv[:, bw + h * dk:bw + (h + 1) * dk]
            vh = qkv[:, 2 * bw + h * dk:2 * bw + (h + 1) * dk]
            qn = qh * (lax.rsqrt(jnp.sum(qh * qh, axis=-1, keepdims=True) + NORM_EPS) * (dk ** -0.5))
            kn = kh * lax.rsqrt(jnp.sum(kh * kh, axis=-1, keepdims=True) + NORM_EPS)
            gl = B_HEADS + h
            bh = beta[:, h:h + 1]
            gch = gc[:, gl:gl + 1]
            glast = gc[c - 1:c, gl:gl + 1]
            diff = gch - gcT[gl:gl + 1, :]
            eg = jnp.exp(gch)
            kb = kn * bh
            lhs_ref[i] = jnp.concatenate([kb, qn], axis=0).astype(BF16)
            kf_ref[i] = kn.astype(BF16)
            dec_ref[i] = jnp.where(lower, jnp.exp(jnp.where(lower, diff, 0.0)), 0.0)
            x0_ref[i] = jnp.concatenate([vh * bh, kb * eg], axis=1)
            qeg_ref[i] = (qn * eg).astype(BF16)
            kdec_ref[i] = (kn * jnp.exp(glast - gch)).astype(BF16)
            egl_ref[i] = jnp.broadcast_to(jnp.exp(glast), egl_ref.shape[1:])

    kq = [lax.dot_general(lhs[i], kfs[i], nt, preferred_element_type=F32) for i in chains]
    a_mats = [jnp.where(strict, kq[i][:c] * decays[i], 0.0) for i in chains]
    qks = [jnp.where(lower, kq[i][c:] * decays[i], 0.0).astype(BF16) for i in chains]
    same = lambda n: (r // n) == (q_ // n)
    eye = (r == q_).astype(F32)
    base = same(SOLVE_BASE)
    bds = [jnp.where(base, a, 0.0) for a in a_mats]
    pws = [b.astype(BF16) for b in bds]
    ts = [eye - b for b in bds]
    for _ in range(int(math.log2(SOLVE_BASE)) - 1):
        pw32 = [_dot(p, p) for p in pws]
        pws = [p.astype(BF16) for p in pw32]
        ts = [_dot(ts[i].astype(BF16), (eye + pw32[i]).astype(BF16)) for i in chains]
    size = SOLVE_BASE
    while size < c:
        sub = same(2 * size) & ((r // size) % 2 == 1) & ((q_ // size) % 2 == 0)
        tb = [t.astype(BF16) for t in ts]
        lt = [_dot(jnp.where(sub, a_mats[i], 0.0).astype(BF16), tb[i]).astype(BF16) for i in chains]
        ts = [ts[i] - _dot(tb[i], lt[i]) for i in chains]
        size *= 2
    xs = [_dot(ts[i].astype(BF16), xs[i].astype(BF16)) for i in chains]
    states = [state_ref[i] for i in chains]
    wq = [jnp.concatenate([xs[i][:, dk:].astype(BF16), qeg[i]], axis=0) for i in chains]
    ws = [_dot(wq[i], states[i].astype(BF16)) for i in chains]
    v_new = [(xs[i][:, :dk] - ws[i][:c]).astype(BF16) for i in chains]
    for i in chains:
        state_ref[i] = states[i] * egl[i] + lax.dot_general(kdec[i], v_new[i], tn, preferred_element_type=F32)
    outs = [ws[i][c:] + _dot(qks[i], v_new[i]) for i in chains]
    onorm = onorm_ref[...]
    for s_i in range(nb):
        z = z_ref[s_i].astype(F32)
        for h in range(B_HEADS):
            o = outs[s_i * B_HEADS + h]
            o = o * lax.rsqrt(jnp.mean(o * o, axis=-1, keepdims=True) + NORM_EPS) * onorm
            o_ref[s_i, :, h * dk:(h + 1) * dk] = (o * _silu(z[:, h * dk:(h + 1) * dk])).astype(o_ref.dtype)


def _gdn(qkv, z, ba, a_log, dt_bias, onorm, batch, seq):
    c = GDN_CHUNK
    dk = B_HEAD_DIM
    bw = B_HEADS * dk
    nb = 2 if batch % 2 == 0 else 1
    nchunk = seq // c
    nch = nb * B_HEADS
    pad = lambda v: jnp.pad(v.astype(F32), (B_HEADS, LANES - 2 * B_HEADS))
    alog, dtb = pad(a_log), pad(dt_bias)
    prep = lambda w: pl.BlockSpec((nb, c, w), lambda b, t: (b, jnp.minimum(t, nchunk - 1), 0))
    done = lambda w: pl.BlockSpec((nb, c, w), lambda b, t: (b, jnp.maximum(t - 1, 0), 0))
    out = pl.pallas_call(
        functools.partial(_gdn_kernel, nb=nb),
        out_shape=jax.ShapeDtypeStruct((batch, seq, bw), BF16),
        grid=(batch // nb, nchunk + 1),
        in_specs=[prep(3 * bw), done(bw), prep(LANES),
                  _const_spec((1, LANES)), _const_spec((1, LANES)), _const_spec((1, dk))],
        out_specs=done(bw),
        scratch_shapes=[pltpu.VMEM((nch, dk, dk), F32),
                        pltpu.VMEM((nch, 2 * c, dk), BF16),
                        pltpu.VMEM((nch, c, dk), BF16),
                        pltpu.VMEM((nch, c, c), F32),
                        pltpu.VMEM((nch, c, 2 * dk), F32),
                        pltpu.VMEM((nch, c, dk), BF16),
                        pltpu.VMEM((nch, c, dk), BF16),
                        pltpu.VMEM((nch, HALO, LANES), F32)],
        compiler_params=pltpu.CompilerParams(dimension_semantics=("arbitrary", "arbitrary"),
                                             vmem_limit_bytes=VMEM_LIMIT),
        name="gated_deltanet",
    )(qkv.reshape(batch, seq, -1), z.reshape(batch, seq, -1), ba.reshape(batch, seq, -1),
      alog.reshape(1, -1), dtb.reshape(1, -1), onorm.reshape(1, -1))
    return out.reshape(batch * seq, bw)


def _lam_kernel(q1_ref, k1_ref, q2_ref, k2_ref, o_ref, *, lambda_init):
    s1 = jnp.sum(q1_ref[...] * k1_ref[...], axis=-1, keepdims=True)
    s2 = jnp.sum(q2_ref[...] * k2_ref[...], axis=-1, keepdims=True)
    o_ref[...] = jnp.exp(s1) - jnp.exp(s2) + lambda_init


def _lambda(q1, k1, q2, k2, lambda_init):
    spec = pl.BlockSpec((1, q1.shape[-1]), lambda: (0, 0))
    return pl.pallas_call(
        functools.partial(_lam_kernel, lambda_init=lambda_init),
        out_shape=jax.ShapeDtypeStruct((1, 1), F32),
        in_specs=[spec] * 4,
        out_specs=pl.BlockSpec((1, 1), lambda: (0, 0)),
        name="lambda",
    )(q1.reshape(1, -1), k1.reshape(1, -1), q2.reshape(1, -1), k2.reshape(1, -1))


def _pick_tile(seq, pref):
    t = min(seq, pref)
    assert seq % t == 0
    return t


def kernel(x, positions, norm_mix, w_in, lambda_q1, lambda_k1, lambda_q2, lambda_k2, a_subln, w_a_out, conv_qkv,
           a_log, dt_bias, b_onorm, w_b_out, w_o, norm_ffn, w_up, ffn_conv, ffn_conv_bias, w_down, norm_final):
    batch, seq, d = x.shape
    depth = w_in.shape[0]
    tile = _pick_tile(seq, ROW_TILE)
    x2d = x.reshape(batch * seq, d)
    tables = _rope_tables(positions, tile)
    for layer in range(depth):
        lambda_init = 0.8 - 0.6 * math.exp(-0.3 * layer)
        qT, k2d, vT, qkv, z, ba, gates = _inproj(x2d, norm_mix[layer].reshape(1, d), tables, w_in[layer],
                                                 conv_qkv[layer], batch, seq, tile)
        lam = _lambda(lambda_q1[layer], lambda_k1[layer], lambda_q2[layer], lambda_k2[layer], lambda_init)
        subln = (a_subln[layer] * (1.0 - lambda_init)).reshape(-1, 1)
        oa = _attention(lam, qT, k2d, vT, subln, batch, seq, tile)
        ob = _gdn(qkv, z, ba, a_log[layer], dt_bias[layer], b_onorm[layer], batch, seq)
        x2d = _merge(x2d, oa, ob, gates, w_a_out[layer], w_b_out[layer], w_o[layer], tile)
        dff = w_down.shape[1]
        x2d = _ffn(x2d, norm_ffn[layer].reshape(1, d), w_up[layer], ffn_conv[layer],
                   ffn_conv_bias[layer].reshape(1, -1), w_down[layer], norm_final.reshape(1, d), seq, tile,
                   fchunk=MXU_COLS if dff % MXU_COLS == 0 else LANES, apply_final=layer == depth - 1)
    return x2d.reshape(batch, seq, d)
```

```python
import functools
import math

import jax
import jax.numpy as jnp
from jax import lax
from jax.experimental import pallas as pl
from jax.experimental.pallas import tpu as pltpu

NORM_EPS = 1e-6
ROPE_THETA = 10000.0
LANES = 128
MXU_COLS = 256
LOG2E = math.log2(math.e)
VMEM_LIMIT = 56 * 1024 * 1024
ROW_TILE = 512

A_HEADS = 8
A_HEAD_DIM = 64
A_V_DIM = 128
B_HEADS = 8
B_HEAD_DIM = 128
CONV_WIDTH = 4
FFN_CONV_WIDTH = 3
GDN_CHUNK = 128
CONV_SPLIT = 2
SOLVE_BASE = 8
HALO = 8
F32 = jnp.float32
BF16 = jnp.bfloat16


def _dot(a, b):
    return jnp.dot(a, b, preferred_element_type=F32)


def _const_spec(shape):
    nd = len(shape)
    return pl.BlockSpec(shape, lambda *_: (0,) * nd, pipeline_mode=pl.Buffered(1))


def _sigmoid(x):
    return 1.0 / (1.0 + jnp.exp(-x))


def _silu(x):
    return x * _sigmoid(x)


def _rope_kernel(pos_ref, freq_ref, cos_ref, sin_ref):
    ang = freq_ref[...] * pos_ref[...].astype(F32)
    cos_ref[...] = jnp.cos(ang)
    sin_ref[...] = jnp.sin(ang)


def _rope_tables(positions, tile):
    n = positions.size
    half = A_HEAD_DIM // 2
    inv_freq = ROPE_THETA ** (-jnp.arange(0, A_HEAD_DIM, 2, dtype=F32) / A_HEAD_DIM)
    col = pl.BlockSpec((half, tile), lambda i: (0, i))
    return pl.pallas_call(
        _rope_kernel,
        out_shape=(jax.ShapeDtypeStruct((half, n), F32),) * 2,
        grid=(n // tile,),
        in_specs=[pl.BlockSpec((1, tile), lambda i: (0, i)), pl.BlockSpec((half, 1), lambda i: (0, 0))],
        out_specs=(col, col),
        name="rope_tables",
    )(positions.reshape(1, n), inv_freq.reshape(half, 1))


def _rotary_t(y, cos_t, sin_t, scale):
    half = A_HEAD_DIM // 2
    outs = []
    for g in range(y.shape[0] // A_HEAD_DIM):
        x1 = y[g * A_HEAD_DIM:g * A_HEAD_DIM + half]
        x2 = y[g * A_HEAD_DIM + half:(g + 1) * A_HEAD_DIM]
        outs.append((x1 * cos_t - x2 * sin_t) * scale)
        outs.append((x2 * cos_t + x1 * sin_t) * scale)
    return jnp.concatenate(outs, axis=0)


def _dot_nt(a, b):
    return lax.dot_general(a, b, (((1,), (1,)), ((), ())), preferred_element_type=F32)


def _inproj_kernel(x_ref, nw_ref, cosT_ref, sinT_ref, wqT_ref, wkT_ref, wvT_ref, wpre_ref, wz_ref, wba_ref,
                   wgate_ref, cw_ref, qT_ref, k_ref, vT_ref, qkv_ref, z_ref, ba_ref, gate_ref, carry_ref, *,
                   tiles_per_seq, chunk):
    tile = x_ref.shape[0]

    @pl.when(pl.program_id(0) % tiles_per_seq == 0)
    def _():
        carry_ref[...] = jnp.zeros(carry_ref.shape, F32)

    x = x_ref[...]
    h = (x * lax.rsqrt(jnp.mean(x * x, axis=-1, keepdims=True) + NORM_EPS) * nw_ref[...]).astype(BF16)
    ba_ref[...] = _dot(h, wba_ref[...])

    cos_t, sin_t = cosT_ref[...], sinT_ref[...]

    def q_job(sl):
        def fin(u):
            qT_ref[0, 0, sl, :] = _rotary_t(u, cos_t, sin_t, A_HEAD_DIM ** -0.5 * LOG2E).astype(BF16)
        return (lambda: _dot_nt(wqT_ref[sl, :], h)), fin

    def v_job(sl):
        def fin(u):
            vT_ref[0, 0, sl, :] = u.astype(BF16)
        return (lambda: _dot_nt(wvT_ref[sl, :], h)), fin

    def k_job(sl):
        def fin(u):
            k_ref[:, sl] = _rotary_t(u, cos_t, sin_t, 1.0).T.astype(BF16)
        return (lambda: _dot_nt(wkT_ref[sl, :], h)), fin

    def plain_job(w_ref, o_ref, sl):
        def fin(u):
            o_ref[:, sl] = u.astype(o_ref.dtype)
        return (lambda: _dot(h, w_ref[:, sl])), fin

    def conv_job(sl, r0, nrows):
        def fin(u):
            ext = jnp.concatenate([carry_ref[:, sl], u], axis=0)
            cw = cw_ref[:, sl]
            y = cw[CONV_WIDTH - 1:CONV_WIDTH] * u
            for s in range(1, CONV_WIDTH):
                y = y + cw[CONV_WIDTH - 1 - s:CONV_WIDTH - s] * pltpu.roll(ext, s, axis=0)[HALO:]
            carry_ref[:, sl] = u[nrows - HALO:, :]
            qkv_ref[r0:r0 + nrows, sl] = _silu(y).astype(qkv_ref.dtype)
        return (lambda: _dot(h[r0:r0 + nrows], wpre_ref[:, sl])), fin

    chunks = lambda width: [slice(c * chunk, (c + 1) * chunk) for c in range(width // chunk)]
    light = ([q_job(sl) for sl in chunks(wqT_ref.shape[0])] + [k_job(sl) for sl in chunks(wkT_ref.shape[0])]
             + [v_job(sl) for sl in chunks(wvT_ref.shape[0])]
             + [plain_job(wz_ref, z_ref, sl) for sl in chunks(wz_ref.shape[1])]
             + [plain_job(wgate_ref, gate_ref, sl) for sl in chunks(wgate_ref.shape[1])])
    heavy = [conv_job(sl, r0, tile // CONV_SPLIT) for sl in chunks(wpre_ref.shape[1])
             for r0 in range(0, tile, tile // CONV_SPLIT)]
    per = max(1, len(light) // len(heavy))
    jobs = []
    for n, hv in enumerate(heavy):
        jobs += light[n * per:(n + 1) * per] + [hv]
    jobs += light[len(heavy) * per:]
    u_next = jobs[0][0]()
    for n, (_, fin) in enumerate(jobs):
        u = u_next
        if n + 1 < len(jobs):
            u_next = jobs[n + 1][0]()
        fin(u)


def _inproj(x2d, nw, tables, w_in, conv_w, batch, seq, tile):
    n, d = x2d.shape
    nt = seq // tile
    aw = A_HEADS * 2 * A_HEAD_DIM
    bw = B_HEADS * B_HEAD_DIM
    half = A_HEAD_DIM // 2
    cos_t, sin_t = tables
    wb = w_in.astype(BF16)
    o = 0
    wq_t = wb[:, o:o + aw].T; o += aw
    wk_t = wb[:, o:o + aw].T; o += aw
    wv_t = wb[:, o:o + aw].T; o += aw
    wpre = wb[:, o:o + 3 * bw]; o += 3 * bw
    wz = wb[:, o:o + bw]; o += bw
    wba = jnp.pad(wb[:, o:o + 2 * B_HEADS], ((0, 0), (0, LANES - 2 * B_HEADS))); o += 2 * B_HEADS
    wgate = wb[:, o:o + 2 * d]
    row = lambda w: pl.BlockSpec((tile, w), lambda i: (i, 0))
    col = pl.BlockSpec((half, tile), lambda i: (0, i))
    tspec = pl.BlockSpec((1, 1, aw, tile), lambda i: (i // nt, i % nt, 0, 0))
    chunk = MXU_COLS if bw % MXU_COLS == 0 and d % MXU_COLS == 0 else LANES
    return pl.pallas_call(
        functools.partial(_inproj_kernel, tiles_per_seq=nt, chunk=chunk),
        out_shape=(jax.ShapeDtypeStruct((batch, nt, aw, tile), BF16),
                   jax.ShapeDtypeStruct((n, aw), BF16),
                   jax.ShapeDtypeStruct((batch, nt, aw, tile), BF16),
                   jax.ShapeDtypeStruct((n, 3 * bw), BF16),
                   jax.ShapeDtypeStruct((n, bw), BF16),
                   jax.ShapeDtypeStruct((n, LANES), F32),
                   jax.ShapeDtypeStruct((n, 2 * d), BF16)),
        grid=(n // tile,),
        in_specs=[row(d), _const_spec((1, d)), col, col,
                  _const_spec(wq_t.shape), _const_spec(wk_t.shape), _const_spec(wv_t.shape),
                  _const_spec(wpre.shape), _const_spec(wz.shape), _const_spec(wba.shape),
                  _const_spec(wgate.shape), _const_spec(conv_w.shape)],
        out_specs=(tspec, row(aw), tspec, row(3 * bw), row(bw), row(LANES), row(2 * d)),
        scratch_shapes=[pltpu.VMEM((HALO, 3 * bw), F32)],
        compiler_params=pltpu.CompilerParams(dimension_semantics=("arbitrary",), vmem_limit_bytes=VMEM_LIMIT),
        name="inproj",
    )(x2d, nw, cos_t, sin_t, wq_t, wk_t, wv_t, wpre, wz, wba, wgate, conv_w)


def _attn_kernel(lam_ref, qT_ref, k_ref, vT_ref, subln_ref, o_ref, qm_ref, m_ref, l_ref, acc_ref, s_ref, mt_ref, *,
                 tile, cb, hp, nt):
    dh = A_HEAD_DIM
    dv = A_V_DIM
    ncb = 2 * tile // cb
    zero = jnp.zeros((dh, tile), BF16)

    def load_q(i):
        for hh in range(hp):
            qT = qT_ref[0, i, hh * 2 * dh:(hh + 1) * 2 * dh, :]
            qm_ref[:, hh * 2 * tile:hh * 2 * tile + tile] = jnp.concatenate([qT[:dh], zero], axis=0)
            qm_ref[:, hh * 2 * tile + tile:(hh + 1) * 2 * tile] = jnp.concatenate([zero, qT[dh:]], axis=0)

    def reset(first):
        m_ref[...] = jnp.full(m_ref.shape, -jnp.inf, F32)
        if first:
            l_ref[...] = jnp.zeros(l_ref.shape, F32)
            acc_ref[...] = jnp.zeros(acc_ref.shape, F32)

    def step(cur, nxt, nxt_diag, cur_diag):
        def rows(c, diag):
            return min(tile, (c * cb) % tile + cb) if diag else tile

        def scores(c):
            hh = c // ncb
            k_next = k_ref[0, pl.ds(pl.multiple_of(nxt * tile, tile), rows(c, nxt_diag)),
                           hh * 2 * dh:(hh + 1) * 2 * dh]
            return _dot(k_next, qm_ref[:, c * cb:(c + 1) * cb])

        s_ahead = scores(0) if nxt is not None else None
        for c in range(hp * ncb):
            hh = c // ncb
            sl = slice(c * cb, (c + 1) * cb)
            qo = (c * cb) % tile
            rows_n, rows_c = rows(c, nxt_diag), rows(c, cur_diag)
            s_new = s_ahead
            if nxt is not None and c + 1 < hp * ncb:
                s_ahead = scores(c + 1)
            if cur is not None:
                m_prev = m_ref[:, sl]
                m_new = jnp.maximum(m_prev, mt_ref[:, sl])
                alpha = jnp.exp2(m_prev - m_new)
                p = jnp.exp2(s_ref[:rows_c, sl] - m_new)
                l_ref[:, sl] = alpha * l_ref[:, sl] + jnp.sum(p, axis=0, keepdims=True)
                v_cur = vT_ref[0, cur, hh * dv:(hh + 1) * dv, :rows_c]
                acc_ref[:, sl] = alpha * acc_ref[:, sl] + _dot(v_cur, p.astype(BF16))
                m_ref[:, sl] = m_new
            if nxt is not None:
                if nxt_diag:
                    kpos = lax.broadcasted_iota(jnp.int32, (rows_n, cb), 0)
                    qpos = lax.broadcasted_iota(jnp.int32, (rows_n, cb), 1) + qo
                    s_new = jnp.where(kpos <= qpos, s_new, -jnp.inf)
                s_ref[:rows_n, sl] = s_new
                mt_ref[:, sl] = jnp.max(s_new, axis=0, keepdims=True)

    def finalize(i):
        lam = lam_ref[0, 0]
        inv_l = 1.0 / l_ref[...]
        rows = pl.ds(pl.multiple_of(i * tile, tile), tile)
        for hh in range(hp):
            c1 = slice(hh * 2 * tile, hh * 2 * tile + tile)
            c2 = slice(hh * 2 * tile + tile, (hh + 1) * 2 * tile)
            o = acc_ref[:, c1] * inv_l[:, c1] - lam * (acc_ref[:, c2] * inv_l[:, c2])
            o = o * lax.rsqrt(jnp.mean(o * o, axis=0, keepdims=True) + NORM_EPS) * subln_ref[...]
            o_ref[rows, hh * dv:(hh + 1) * dv] = o.T.astype(o_ref.dtype)

    load_q(0)
    reset(True)
    step(None, 0, True, False)

    def q_body(i, carry):
        def kv_body(j, c):
            step(j, j + 1, False, False)
            return c

        lax.fori_loop(0, i - 1, kv_body, 0)

        @pl.when(i > 0)
        def _():
            step(i - 1, i, True, False)

        @pl.when(i < nt - 1)
        def _():
            load_q(i + 1)
            step(i, 0, False, True)

        @pl.when(i == nt - 1)
        def _():
            step(i, None, False, True)

        finalize(i)
        reset(False)
        return carry

    lax.fori_loop(0, nt, q_body, 0)


def _attention(lam, qT, k2d, vT, subln_col, batch, seq, tile):
    nt = seq // tile
    dv = A_V_DIM
    n = batch * seq
    hp = 4
    assert A_HEADS % hp == 0
    kern = functools.partial(_attn_kernel, tile=tile, cb=min(tile, MXU_COLS), hp=hp, nt=nt)
    lanes = hp * 2 * tile
    return pl.pallas_call(
        kern,
        out_shape=jax.ShapeDtypeStruct((n, A_HEADS * dv), BF16),
        grid=(batch, A_HEADS // hp),
        in_specs=[pl.BlockSpec(memory_space=pltpu.SMEM),
                  pl.BlockSpec((1, nt, hp * 2 * A_HEAD_DIM, tile), lambda b, h: (b, 0, h, 0)),
                  pl.BlockSpec((1, seq, hp * 2 * A_HEAD_DIM), lambda b, h: (b, 0, h)),
                  pl.BlockSpec((1, nt, hp * dv, tile), lambda b, h: (b, 0, h, 0)),
                  pl.BlockSpec((dv, 1), lambda b, h: (0, 0))],
        out_specs=pl.BlockSpec((seq, hp * dv), lambda b, h: (b, h)),
        scratch_shapes=[pltpu.VMEM((2 * A_HEAD_DIM, lanes), BF16),
                        pltpu.VMEM((1, lanes), F32),
                        pltpu.VMEM((1, lanes), F32),
                        pltpu.VMEM((dv, lanes + LANES), F32),
                        pltpu.VMEM((tile, lanes + LANES), F32),
                        pltpu.VMEM((1, lanes), F32)],
        compiler_params=pltpu.CompilerParams(dimension_semantics=("arbitrary",) * 2, vmem_limit_bytes=VMEM_LIMIT),
        name="diff_attention",
    )(lam, qT, k2d.reshape(batch, seq, -1), vT, subln_col)


def _merge_kernel(x_ref, oa_ref, ob_ref, gate_ref, wa_ref, wb_ref, wo_ref, o_ref):
    d = x_ref.shape[1]
    ya = _dot(oa_ref[...], wa_ref[...])
    yb = _dot(ob_ref[...], wb_ref[...])
    g = gate_ref[...].astype(F32)
    merged = _sigmoid(g[:, :d]) * ya + _sigmoid(g[:, d:]) * yb
    o_ref[...] = x_ref[...] + _dot(merged.astype(BF16), wo_ref[...])


def _merge(x2d, oa, ob, gates, wa, wb, wo, tile):
    n, d = x2d.shape
    row = lambda w: pl.BlockSpec((tile, w), lambda i: (i, 0))
    return pl.pallas_call(
        _merge_kernel,
        out_shape=jax.ShapeDtypeStruct((n, d), F32),
        grid=(n // tile,),
        in_specs=[row(d), row(oa.shape[1]), row(ob.shape[1]), row(2 * d),
                  _const_spec(wa.shape), _const_spec(wb.shape), _const_spec(wo.shape)],
        out_specs=row(d),
        compiler_params=pltpu.CompilerParams(dimension_semantics=("arbitrary",), vmem_limit_bytes=VMEM_LIMIT),
        name="merge_outproj",
    )(x2d, oa, ob, gates, wa.astype(BF16), wb.astype(BF16), wo.astype(BF16))


def _ffn_kernel(x_ref, nw_ref, wup_ref, cw_ref, cb_ref, wdown_ref, nf_ref, o_ref, ucar_ref, act_ref, *, tiles_per_seq,
                fchunk, apply_final):
    i = pl.program_id(0)
    tile, d = x_ref.shape
    dff = wdown_ref.shape[0]

    @pl.when(i % tiles_per_seq == 0)
    def _():
        ucar_ref[...] = jnp.zeros(ucar_ref.shape, F32)

    x = x_ref[...]
    h = (x * lax.rsqrt(jnp.mean(x * x, axis=-1, keepdims=True) + NORM_EPS) * nw_ref[...]).astype(BF16)

    nrows = tile // CONV_SPLIT
    jobs = [(c, r0) for c in range(dff // fchunk) for r0 in range(0, tile, nrows)]

    def up(job):
        c, r0 = job
        return [_dot(h[r0:r0 + nrows], wup_ref[:, base + c * fchunk:base + (c + 1) * fchunk]) for base in (0, dff)]

    u_next = up(jobs[0])
    for n, (c, r0) in enumerate(jobs):
        u_cur = u_next
        if n + 1 < len(jobs):
            u_next = up(jobs[n + 1])
        halves = []
        for u, base in zip(u_cur, (0, dff)):
            sl = slice(base + c * fchunk, base + (c + 1) * fchunk)
            uext = jnp.concatenate([ucar_ref[:, sl], u], axis=0)
            cw = cw_ref[:, sl]
            y = cb_ref[:, sl] + cw[FFN_CONV_WIDTH - 1:FFN_CONV_WIDTH] * u
            for s in range(1, FFN_CONV_WIDTH):
                y = y + cw[FFN_CONV_WIDTH - 1 - s:FFN_CONV_WIDTH - s] * pltpu.roll(uext, s, axis=0)[HALO:]
            ucar_ref[:, sl] = u[nrows - HALO:, :]
            halves.append(y)
        act_ref[r0:r0 + nrows, c * fchunk:(c + 1) * fchunk] = (_silu(halves[0]) * halves[1]).astype(BF16)
    x2 = x + _dot(act_ref[...], wdown_ref[...])
    if apply_final:
        x2 = x2 * lax.rsqrt(jnp.mean(x2 * x2, axis=-1, keepdims=True) + NORM_EPS) * nf_ref[...]
    o_ref[...] = x2


def _ffn(x2d, nw, wup, cw, cb, wdown, nf, seq, tile, fchunk, apply_final):
    n, d = x2d.shape
    dff = wdown.shape[0]
    row = pl.BlockSpec((tile, d), lambda i: (i, 0))
    kern = functools.partial(_ffn_kernel, tiles_per_seq=seq // tile, fchunk=fchunk, apply_final=apply_final)
    return pl.pallas_call(
        kern,
        out_shape=jax.ShapeDtypeStruct((n, d), F32),
        grid=(n // tile,),
        in_specs=[row, _const_spec((1, d)), _const_spec(wup.shape), _const_spec(cw.shape),
                  _const_spec((1, 2 * dff)), _const_spec(wdown.shape), _const_spec((1, d))],
        out_specs=row,
        scratch_shapes=[pltpu.VMEM((HALO, 2 * dff), F32), pltpu.VMEM((tile, dff), BF16)],
        compiler_params=pltpu.CompilerParams(dimension_semantics=("arbitrary",), vmem_limit_bytes=VMEM_LIMIT),
        name="convffn",
    )(x2d, nw, wup.astype(BF16), cw, cb, wdown.astype(BF16), nf)


def _gdn_kernel(qkv_ref, z_ref, ba_ref, alog_ref, dtb_ref, onorm_ref, o_ref,
                state_ref, lhs_ref, kf_ref, dec_ref, x0_ref, qeg_ref, kdec_ref, egl_ref, *, nb):
    c = GDN_CHUNK
    dk = B_HEAD_DIM
    bw = B_HEADS * dk
    chains = range(nb * B_HEADS)
    prepared = (lhs_ref, kf_ref, dec_ref, x0_ref, qeg_ref, kdec_ref, egl_ref)

    @pl.when(pl.program_id(1) == 0)
    def _():
        state_ref[...] = jnp.zeros(state_ref.shape, F32)
        for ref in prepared:
            ref[...] = jnp.zeros(ref.shape, ref.dtype)

    r = lax.broadcasted_iota(jnp.int32, (c, c), 0)
    q_ = lax.broadcasted_iota(jnp.int32, (c, c), 1)
    lower = r >= q_
    strict = r > q_
    tri_lo = lower.astype(BF16)
    softplus = lambda t: jnp.maximum(t, 0.0) + jnp.log1p(jnp.exp(-jnp.abs(t)))
    nt = (((1,), (1,)), ((), ()))
    tn = (((0,), (0,)), ((), ()))

    lhs = [lhs_ref[i] for i in chains]
    kfs = [kf_ref[i] for i in chains]
    decays = [dec_ref[i] for i in chains]
    xs = [x0_ref[i] for i in chains]
    qeg = [qeg_ref[i] for i in chains]
    kdec = [kdec_ref[i] for i in chains]
    egl = [egl_ref[i][0:1, :] for i in chains]

    for s_i in range(nb):
        qkv = qkv_ref[s_i].astype(F32)
        ba = ba_ref[s_i]
        beta = _sigmoid(ba)
        g = -jnp.exp(alog_ref[...]) * softplus(ba + dtb_ref[...])
        g_hi = g.astype(BF16)
        rest = g - g_hi.astype(F32)
        g_mid = rest.astype(BF16)
        g_lo = (rest - g_mid.astype(F32)).astype(BF16)
        gc = _dot(tri_lo, g_hi) + _dot(tri_lo, g_mid) + _dot(tri_lo, g_lo)
        gcT = gc.T
        for h in range(B_HEADS):
            i = s_i * B_HEADS + h
            qh = qkv[:, h * dk:(h + 1) * dk]
            kh = qkv[:, bw + h * dk:bw + (h + 1) * dk]
            vh = qkv[:, 2 * bw + h * dk:2 * bw + (h + 1) * dk]
            qn = qh * (lax.rsqrt(jnp.sum(qh * qh, axis=-1, keepdims=True) + NORM_EPS) * (dk ** -0.5))
            kn = kh * lax.rsqrt(jnp.sum(kh * kh, axis=-1, keepdims=True) + NORM_EPS)
            gl = B_HEADS + h
            bh = beta[:, h:h + 1]
            gch = gc[:, gl:gl + 1]
            glast = gc[c - 1:c, gl:gl + 1]
            diff = gch - gcT[gl:gl + 1, :]
            eg = jnp.exp(gch)
            kb = kn * bh
            lhs_ref[i] = jnp.concatenate([kb, qn], axis=0).astype(BF16)
            kf_ref[i] = kn.astype(BF16)
            dec_ref[i] = jnp.where(lower, jnp.exp(jnp.where(lower, diff, 0.0)), 0.0)
            x0_ref[i] = jnp.concatenate([vh * bh, kb * eg], axis=1)
            qeg_ref[i] = (qn * eg).astype(BF16)
            kdec_ref[i] = (kn * jnp.exp(glast - gch)).astype(BF16)
            egl_ref[i] = jnp.broadcast_to(jnp.exp(glast), egl_ref.shape[1:])

    kq = [lax.dot_general(lhs[i], kfs[i], nt, preferred_element_type=F32) for i in chains]
    a_mats = [jnp.where(strict, kq[i][:c] * decays[i], 0.0) for i in chains]
    qks = [jnp.where(lower, kq[i][c:] * decays[i], 0.0).astype(BF16) for i in chains]
    same = lambda n: (r // n) == (q_ // n)
    eye = (r == q_).astype(F32)
    base = same(SOLVE_BASE)
    bds = [jnp.where(base, a, 0.0) for a in a_mats]
    pws = [b.astype(BF16) for b in bds]
    ts = [eye - b for b in bds]
    for _ in range(int(math.log2(SOLVE_BASE)) - 1):
        pw32 = [_dot(p, p) for p in pws]
        pws = [p.astype(BF16) for p in pw32]
        ts = [_dot(ts[i].astype(BF16), (eye + pw32[i]).astype(BF16)) for i in chains]
    size = SOLVE_BASE
    while size < c:
        sub = same(2 * size) & ((r // size) % 2 == 1) & ((q_ // size) % 2 == 0)
        tb = [t.astype(BF16) for t in ts]
        lt = [_dot(jnp.where(sub, a_mats[i], 0.0).astype(BF16), tb[i]).astype(BF16) for i in chains]
        ts = [ts[i] - _dot(tb[i], lt[i]) for i in chains]
        size *= 2
    xs = [_dot(ts[i].astype(BF16), xs[i].astype(BF16)) for i in chains]
    states = [state_ref[i] for i in chains]
    wq = [jnp.concatenate([xs[i][:, dk:].astype(BF16), qeg[i]], axis=0) for i in chains]
    ws = [_dot(wq[i], states[i].astype(BF16)) for i in chains]
    v_new = [(xs[i][:, :dk] - ws[i][:c]).astype(BF16) for i in chains]
    for i in chains:
        state_ref[i] = states[i] * egl[i] + lax.dot_general(kdec[i], v_new[i], tn, preferred_element_type=F32)
    outs = [ws[i][c:] + _dot(qks[i], v_new[i]) for i in chains]
    onorm = onorm_ref[...]
    for s_i in range(nb):
        z = z_ref[s_i].astype(F32)
        for h in range(B_HEADS):
            o = outs[s_i * B_HEADS + h]
            o = o * lax.rsqrt(jnp.mean(o * o, axis=-1, keepdims=True) + NORM_EPS) * onorm
            o_ref[s_i, :, h * dk:(h + 1) * dk] = (o * _silu(z[:, h * dk:(h + 1) * dk])).astype(o_ref.dtype)


def _gdn(qkv, z, ba, a_log, dt_bias, onorm, batch, seq):
    c = GDN_CHUNK
    dk = B_HEAD_DIM
    bw = B_HEADS * dk
    nb = 2 if batch % 2 == 0 else 1
    nchunk = seq // c
    nch = nb * B_HEADS
    pad = lambda v: jnp.pad(v.astype(F32), (B_HEADS, LANES - 2 * B_HEADS))
    alog, dtb = pad(a_log), pad(dt_bias)
    prep = lambda w: pl.BlockSpec((nb, c, w), lambda b, t: (b, jnp.minimum(t, nchunk - 1), 0))
    done = lambda w: pl.BlockSpec((nb, c, w), lambda b, t: (b, jnp.maximum(t - 1, 0), 0))
    out = pl.pallas_call(
        functools.partial(_gdn_kernel, nb=nb),
        out_shape=jax.ShapeDtypeStruct((batch, seq, bw), BF16),
        grid=(batch // nb, nchunk + 1),
        in_specs=[prep(3 * bw), done(bw), prep(LANES),
                  _const_spec((1, LANES)), _const_spec((1, LANES)), _const_spec((1, dk))],
        out_specs=done(bw),
        scratch_shapes=[pltpu.VMEM((nch, dk, dk), F32),
                        pltpu.VMEM((nch, 2 * c, dk), BF16),
                        pltpu.VMEM((nch, c, dk), BF16),
                        pltpu.VMEM((nch, c, c), F32),
                        pltpu.VMEM((nch, c, 2 * dk), F32),
                        pltpu.VMEM((nch, c, dk), BF16),
                        pltpu.VMEM((nch, c, dk), BF16),
                        pltpu.VMEM((nch, HALO, LANES), F32)],
        compiler_params=pltpu.CompilerParams(dimension_semantics=("arbitrary", "arbitrary"),
                                             vmem_limit_bytes=VMEM_LIMIT),
        name="gated_deltanet",
    )(qkv.reshape(batch, seq, -1), z.reshape(batch, seq, -1), ba.reshape(batch, seq, -1),
      alog.reshape(1, -1), dtb.reshape(1, -1), onorm.reshape(1, -1))
    return out.reshape(batch * seq, bw)


def _lam_kernel(q1_ref, k1_ref, q2_ref, k2_ref, o_ref, *, lambda_init):
    s1 = jnp.sum(q1_ref[...] * k1_ref[...], axis=-1, keepdims=True)
    s2 = jnp.sum(q2_ref[...] * k2_ref[...], axis=-1, keepdims=True)
    o_ref[...] = jnp.exp(s1) - jnp.exp(s2) + lambda_init


def _lambda(q1, k1, q2, k2, lambda_init):
    spec = pl.BlockSpec((1, q1.shape[-1]), lambda: (0, 0))
    return pl.pallas_call(
        functools.partial(_lam_kernel, lambda_init=lambda_init),
        out_shape=jax.ShapeDtypeStruct((1, 1), F32),
        in_specs=[spec] * 4,
        out_specs=pl.BlockSpec((1, 1), lambda: (0, 0)),
        name="lambda",
    )(q1.reshape(1, -1), k1.reshape(1, -1), q2.reshape(1, -1), k2.reshape(1, -1))


def _pick_tile(seq, pref):
    t = min(seq, pref)
    assert seq % t == 0
    return t


def kernel(x, positions, norm_mix, w_in, lambda_q1, lambda_k1, lambda_q2, lambda_k2, a_subln, w_a_out, conv_qkv,
           a_log, dt_bias, b_onorm, w_b_out, w_o, norm_ffn, w_up, ffn_conv, ffn_conv_bias, w_down, norm_final):
    batch, seq, d = x.shape
    depth = w_in.shape[0]
    tile = _pick_tile(seq, ROW_TILE)
    x2d = x.reshape(batch * seq, d)
    tables = _rope_tables(positions, tile)
    for layer in range(depth):
        lambda_init = 0.8 - 0.6 * math.exp(-0.3 * layer)
        qT, k2d, vT, qkv, z, ba, gates = _inproj(x2d, norm_mix[layer].reshape(1, d), tables, w_in[layer],
                                                 conv_qkv[layer], batch, seq, tile)
        lam = _lambda(lambda_q1[layer], lambda_k1[layer], lambda_q2[layer], lambda_k2[layer], lambda_init)
        subln = (a_subln[layer] * (1.0 - lambda_init)).reshape(-1, 1)
        oa = _attention(lam, qT, k2d, vT, subln, batch, seq, tile)
        ob = _gdn(qkv, z, ba, a_log[layer], dt_bias[layer], b_onorm[layer], batch, seq)
        x2d = _merge(x2d, oa, ob, gates, w_a_out[layer], w_b_out[layer], w_o[layer], tile)
        dff = w_down.shape[1]
        x2d = _ffn(x2d, norm_ffn[layer].reshape(1, d), w_up[layer], ffn_conv[layer],
                   ffn_conv_bias[layer].reshape(1, -1), w_down[layer], norm_final.reshape(1, d), seq, tile,
                   fchunk=MXU_COLS if dff % MXU_COLS == 0 else LANES, apply_final=layer == depth - 1)
    return x2d.reshape(batch, seq, d)
```
